```python
import jax, jax.numpy as jnp
from jax import lax
import numpy as np

D_MODEL = 1024
BATCH = 2
SEQ = 8192
DEPTH = 2
DEC_BATCH = 32
DEC_SEQ = 4
PAST_LEN = 8192
PAGE_SIZE = 128

N_MIXERS = 2
N_CONV_LAYERS = (DEPTH + 1) // 2
N_ATTN_LAYERS = DEPTH // 2
CONV_WIDTH = 31
CONV_HIST = CONV_WIDTH - 1
N_HEADS = 16
HEAD_DIM = D_MODEL // N_HEADS
Q_BLOCK = 128
D_FF = ((8 * D_MODEL + 3 * 256 - 1) // (3 * 256)) * 256
EPS = 1e-6
FORGET_BIAS_MEAN = 3.0

kernel_name = "conformer_conv_fox_hybrid_step"


def rmsnorm(x, g):
    xf = x.astype(jnp.float32)
    y = xf * lax.rsqrt(jnp.mean(xf * xf, axis=-1, keepdims=True) + EPS)
    return (y * g.astype(jnp.float32)).astype(x.dtype)


def layernorm(x, g, b):
    xf = x.astype(jnp.float32)
    mu = jnp.mean(xf, axis=-1, keepdims=True)
    var = jnp.mean(jnp.square(xf - mu), axis=-1, keepdims=True)
    y = (xf - mu) * lax.rsqrt(var + EPS)
    return (y * g.astype(jnp.float32) + b.astype(jnp.float32)).astype(x.dtype)


def conv_module(xn, hist, w_in, b_in, w_dw, b_dw, ln_g, ln_b, w_out, b_out):
    h = xn @ w_in + b_in
    a, gate = jnp.split(h, 2, axis=-1)
    u = a * jax.nn.sigmoid(gate)
    upad = jnp.concatenate([hist.astype(u.dtype), u], axis=1)
    y = lax.conv_general_dilated(
        upad, w_dw[:, None, :].astype(u.dtype), window_strides=(1,), padding='VALID',
        dimension_numbers=('NWC', 'WIO', 'NWC'), feature_group_count=u.shape[-1]) + b_dw
    y = jax.nn.silu(layernorm(y, ln_g, ln_b))
    out = y @ w_out + b_out
    return out, upad[:, -CONV_HIST:]


def fox_proj(xn, w_in, b_f):
    B, T, D = xn.shape
    p = xn @ w_in
    q = p[..., :D].reshape(B, T, N_HEADS, HEAD_DIM)
    k = p[..., D:2 * D].reshape(B, T, N_HEADS, HEAD_DIM)
    v = p[..., 2 * D:3 * D].reshape(B, T, N_HEADS, HEAD_DIM)
    logf = jax.nn.log_sigmoid((p[..., 3 * D:] + b_f).astype(jnp.float32))
    return q, k, v, logf


def fox_prompt(q, k, v, logf):
    B, S, H, dh = q.shape
    scale = dh ** -0.5
    c = jnp.cumsum(logf, axis=1)
    cT = jnp.transpose(c, (0, 2, 1))
    nb = S // Q_BLOCK
    qb = jnp.transpose(q.reshape(B, nb, Q_BLOCK, H, dh), (1, 0, 2, 3, 4))
    cqb = jnp.transpose(c.reshape(B, nb, Q_BLOCK, H), (1, 0, 3, 2))
    kpos = jnp.arange(S)

    def block(args):
        i, qi, ci = args
        s = jnp.einsum('bqhd,bkhd->bhqk', qi, k, preferred_element_type=jnp.float32) * scale
        s = s + ci[:, :, :, None] - cT[:, :, None, :]
        qpos = i * Q_BLOCK + jnp.arange(Q_BLOCK)
        s = jnp.where(kpos[None, :] <= qpos[:, None], s, -jnp.inf)
        pr = jax.nn.softmax(s, axis=-1)
        return jnp.einsum('bhqk,bkhd->bqhd', pr.astype(v.dtype), v)

    o = lax.map(block, (jnp.arange(nb), qb, cqb))
    return jnp.transpose(o, (1, 0, 2, 3, 4)).reshape(B, S, H * dh)


def fox_sample(q, k, v, logf, cache_k, cache_v, cache_logf, page_table):
    DB, T, H, dh = q.shape
    scale = dh ** -0.5
    P = page_table.shape[1] * cache_k.shape[1]
    kp = cache_k[page_table].reshape(DB, P, H, dh).astype(k.dtype)
    vp = cache_v[page_table].reshape(DB, P, H, dh).astype(v.dtype)
    lp = cache_logf[page_table].reshape(DB, P, H).astype(jnp.float32)
    k_all = jnp.concatenate([kp, k], axis=1)
    v_all = jnp.concatenate([vp, v], axis=1)
    c = jnp.cumsum(jnp.concatenate([lp, logf], axis=1), axis=1)
    cT = jnp.transpose(c, (0, 2, 1))
    s = jnp.einsum('bqhd,bkhd->bhqk', q, k_all, preferred_element_type=jnp.float32) * scale
    s = s + cT[:, :, P:, None] - cT[:, :, None, :]
    kpos = jnp.arange(P + T)
    qpos = P + jnp.arange(T)
    s = jnp.where(kpos[None, :] <= qpos[:, None], s, -jnp.inf)
    pr = jax.nn.softmax(s, axis=-1)
    o = jnp.einsum('bhqk,bkhd->bqhd', pr.astype(v.dtype), v_all)
    return o.reshape(DB, T, H * dh)


def swiglu(xn, w_gu, w_down):
    g, u = jnp.split(xn @ w_gu, 2, axis=-1)
    return (jax.nn.silu(g) * u) @ w_down


def setup_inputs(seed: int = 0) -> dict:
    key = jax.random.key(seed)
    ks = jax.random.split(key, 24)
    D, H, F, K = D_MODEL, N_HEADS, D_FF, CONV_WIDTH
    NC, NA = N_CONV_LAYERS, N_ATTN_LAYERS
    n_pages = PAST_LEN // PAGE_SIZE
    n_used = DEC_BATCH * n_pages
    n_phys = n_used + max(1, n_used // 4)
    nrm = jax.random.normal
    f32 = jnp.float32
    page_table = jax.random.permutation(ks[0], n_phys)[:n_used].reshape(DEC_BATCH, n_pages).astype(jnp.int32)
    return {
        "x_prompt": nrm(ks[1], (BATCH, SEQ, D), f32),
        "x_sample": nrm(ks[2], (DEC_BATCH, DEC_SEQ, D), f32),
        "state_conv": 0.5 * nrm(ks[3], (NC, DEC_BATCH, CONV_HIST, D), f32),
        "cache_k": nrm(ks[4], (NA, n_phys, PAGE_SIZE, H, HEAD_DIM), f32),
        "cache_v": nrm(ks[5], (NA, n_phys, PAGE_SIZE, H, HEAD_DIM), f32),
        "cache_logf": jax.nn.log_sigmoid(FORGET_BIAS_MEAN + nrm(ks[6], (NA, n_phys, PAGE_SIZE, H), f32)),
        "page_table": page_table,
        "norm_mix_g": 1.0 + 0.02 * nrm(ks[7], (DEPTH, D), f32),
        "norm_ffn_g": 1.0 + 0.02 * nrm(ks[8], (DEPTH, D), f32),
        "norm_final_g": 1.0 + 0.02 * nrm(ks[9], (D,), f32),
        "conv_w_in": nrm(ks[10], (NC, D, 2 * D), f32) * D ** -0.5,
        "conv_b_in": 0.02 * nrm(ks[11], (NC, 2 * D), f32),
        "conv_w_dw": nrm(ks[12], (NC, K, D), f32) * K ** -0.5,
        "conv_b_dw": 0.02 * nrm(ks[13], (NC, D), f32),
        "conv_ln_g": 1.0 + 0.02 * nrm(ks[14], (NC, D), f32),
        "conv_ln_b": 0.02 * nrm(ks[15], (NC, D), f32),
        "conv_w_out": nrm(ks[16], (NC, D, D), f32) * D ** -0.5,
        "conv_b_out": 0.02 * nrm(ks[17], (NC, D), f32),
        "attn_w_in": nrm(ks[18], (NA, D, 3 * D + H), f32) * D ** -0.5,
        "attn_b_f": FORGET_BIAS_MEAN + nrm(ks[19], (NA, H), f32),
        "attn_w_out": nrm(ks[20], (NA, D, D), f32) * D ** -0.5,
        "ffn_w_gu": nrm(ks[21], (DEPTH, D, 2 * F), f32) * D ** -0.5,
        "ffn_w_down": nrm(ks[22], (DEPTH, F, D), f32) * F ** -0.5,
    }


def reference(x_prompt, x_sample, state_conv, cache_k, cache_v, cache_logf, page_table,
              norm_mix_g, norm_ffn_g, norm_final_g,
              conv_w_in, conv_b_in, conv_w_dw, conv_b_dw, conv_ln_g, conv_ln_b, conv_w_out, conv_b_out,
              attn_w_in, attn_b_f, attn_w_out, ffn_w_gu, ffn_w_down):
    hp, hs = x_prompt, x_sample
    conv_p, conv_s = [], []
    kp_l, vp_l, lp_l, ks_l, vs_l, ls_l = [], [], [], [], [], []
    for i in range(DEPTH):
        j = i // N_MIXERS
        xp_n = rmsnorm(hp, norm_mix_g[i])
        xs_n = rmsnorm(hs, norm_mix_g[i])
        if i % N_MIXERS == 0:
            cw = (conv_w_in[j], conv_b_in[j], conv_w_dw[j], conv_b_dw[j],
                  conv_ln_g[j], conv_ln_b[j], conv_w_out[j], conv_b_out[j])
            zero_hist = jnp.zeros((hp.shape[0], CONV_HIST, hp.shape[2]), hp.dtype)
            mp, new_hp = conv_module(xp_n, zero_hist, *cw)
            ms, new_hs = conv_module(xs_n, state_conv[j], *cw)
            conv_p.append(new_hp)
            conv_s.append(new_hs)
        else:
            q, k, v, lf = fox_proj(xp_n, attn_w_in[j], attn_b_f[j])
            mp = fox_prompt(q, k, v, lf) @ attn_w_out[j]
            kp_l.append(k); vp_l.append(v); lp_l.append(lf)
            q, k, v, lf = fox_proj(xs_n, attn_w_in[j], attn_b_f[j])
            ms = fox_sample(q, k, v, lf, cache_k[j], cache_v[j], cache_logf[j], page_table) @ attn_w_out[j]
            ks_l.append(k); vs_l.append(v); ls_l.append(lf)
        hp = hp + mp
        hs = hs + ms
        hp = hp + swiglu(rmsnorm(hp, norm_ffn_g[i]), ffn_w_gu[i], ffn_w_down[i])
        hs = hs + swiglu(rmsnorm(hs, norm_ffn_g[i]), ffn_w_gu[i], ffn_w_down[i])
    y_prompt = rmsnorm(hp, norm_final_g)
    y_sample = rmsnorm(hs, norm_final_g)
    return (y_prompt, y_sample,
            jnp.stack(conv_p, 0), jnp.stack(conv_s, 0),
            jnp.stack(kp_l, 0), jnp.stack(vp_l, 0), jnp.stack(lp_l, 0),
            jnp.stack(ks_l, 0), jnp.stack(vs_l, 0), jnp.stack(ls_l, 0))
```

```python
import functools

import jax
import jax.numpy as jnp
from jax import lax
from jax.experimental import pallas as pl
from jax.experimental.pallas import tpu as pltpu

D_MODEL = 1024
N_HEADS = 16
HEAD_DIM = 64
N_PAIRS = N_HEADS // 2
LANES = 128
D_FF = 2816
CONV_WIDTH = 31
CONV_HIST = CONV_WIDTH - 1
HIST_PAD = 32
EPS = 1e-6
PAGE = 128
NEG = -1e30
VMEM_LIMIT_BYTES = 56 * 1024 * 1024

F32 = jnp.float32
BF16 = jnp.bfloat16
NT_DIMS = (((1,), (1,)), ((), ()))


def _params(*sem):
    return pltpu.CompilerParams(dimension_semantics=sem, vmem_limit_bytes=VMEM_LIMIT_BYTES)


def _resident(shape):
    nd = len(shape)
    return pl.BlockSpec(shape, lambda *_: (0,) * nd, pipeline_mode=pl.Buffered(1))


def _rmsnorm(x, g):
    return x * lax.rsqrt(jnp.mean(x * x, axis=-1, keepdims=True) + EPS) * g


def _sigmoid(x):
    return 1.0 / (1.0 + jnp.exp(-x))


def _log_sigmoid(x):
    return jnp.minimum(x, 0.0) - jnp.log1p(jnp.exp(-jnp.abs(x)))


def _layernorm_silu(y, g, b):
    mu = jnp.mean(y, axis=-1, keepdims=True)
    yc = y - mu
    var = jnp.mean(yc * yc, axis=-1, keepdims=True)
    yn = yc * lax.rsqrt(var + EPS) * g + b
    return yn * _sigmoid(yn)


def _dot(a, b):
    return jnp.dot(a, b, preferred_element_type=F32)


def _dot_nt(a, b):
    return lax.dot_general(a, b, NT_DIMS, preferred_element_type=F32)


def _dot_exact(a, b):
    return jnp.dot(a, b, preferred_element_type=F32, precision=lax.Precision.HIGHEST)


def _upper_tri(n):
    return (lax.broadcasted_iota(jnp.int32, (n, n), 0) <= lax.broadcasted_iota(jnp.int32, (n, n), 1)).astype(F32)


def _conv_in_kernel(h_ref, g_ref, w_ref, b_ref, u_ref):
    xn = _rmsnorm(h_ref[...], g_ref[...]).astype(BF16)
    hh = _dot(xn, w_ref[...]) + b_ref[...]
    u_ref[...] = hh[:, :D_MODEL] * _sigmoid(hh[:, D_MODEL:])


def conv_in(h, g, w_bf, b, *, tm):
    m = h.shape[0]
    return pl.pallas_call(
        _conv_in_kernel,
        out_shape=jax.ShapeDtypeStruct((m, D_MODEL), F32),
        grid=(m // tm,),
        in_specs=[pl.BlockSpec((tm, D_MODEL), lambda i: (i, 0)),
                  _resident((1, D_MODEL)),
                  _resident((D_MODEL, 2 * D_MODEL)),
                  _resident((1, 2 * D_MODEL))],
        out_specs=pl.BlockSpec((tm, D_MODEL), lambda i: (i, 0)),
        compiler_params=_params("parallel"),
        name="conv_in",
    )(h, g, w_bf, b)


def _conv_mid_kernel(u_ref, wdw_ref, bdw_ref, lng_ref, lnb_ref, z_ref, st_ref, ubuf, ybuf, *, tt):
    t = pl.program_id(1)

    @pl.when(t == 0)
    def _():
        ubuf[0:HIST_PAD, :] = jnp.zeros((HIST_PAD, D_MODEL), F32)

    @pl.when(t > 0)
    def _():
        ubuf[0:HIST_PAD, :] = ubuf[tt:tt + HIST_PAD, :]

    ubuf[HIST_PAD:HIST_PAD + tt, :] = u_ref[0]
    st_ref[0] = ubuf[tt + HIST_PAD - CONV_HIST:tt + HIST_PAD, :]

    off = HIST_PAD - CONV_HIST
    rc = min(tt, 64)
    for r0 in range(0, tt, rc):
        for l0 in range(0, D_MODEL, LANES):
            acc = jnp.broadcast_to(bdw_ref[:, l0:l0 + LANES], (rc, LANES))
            for j in range(CONV_WIDTH):
                acc = acc + wdw_ref[j:j + 1, l0:l0 + LANES] * ubuf[r0 + j + off:r0 + j + off + rc, l0:l0 + LANES]
            ybuf[r0:r0 + rc, l0:l0 + LANES] = acc

    z_ref[0] = _layernorm_silu(ybuf[...], lng_ref[...], lnb_ref[...]).astype(z_ref.dtype)


def conv_mid(u, w_dw, b_dw, ln_g, ln_b, *, tt):
    b, t, _ = u.shape
    return pl.pallas_call(
        functools.partial(_conv_mid_kernel, tt=tt),
        out_shape=(jax.ShapeDtypeStruct((b, t, D_MODEL), BF16),
                   jax.ShapeDtypeStruct((b, CONV_HIST, D_MODEL), F32)),
        grid=(b, t // tt),
        in_specs=[pl.BlockSpec((1, tt, D_MODEL), lambda i, j: (i, j, 0)),
                  _resident((CONV_WIDTH, D_MODEL)),
                  _resident((1, D_MODEL)),
                  _resident((1, D_MODEL)),
                  _resident((1, D_MODEL))],
        out_specs=(pl.BlockSpec((1, tt, D_MODEL), lambda i, j: (i, j, 0)),
                   pl.BlockSpec((1, CONV_HIST, D_MODEL), lambda i, j: (i, 0, 0))),
        scratch_shapes=[pltpu.VMEM((HIST_PAD + tt, D_MODEL), F32),
                        pltpu.VMEM((tt, D_MODEL), F32)],
        compiler_params=_params("parallel", "arbitrary"),
        name="conv_mid",
    )(u, w_dw, b_dw, ln_g, ln_b)


def _conv_mid_sample_kernel(u_ref, hist_ref, wdw_ref, bdw_ref, lng_ref, lnb_ref, z_ref, st_ref, upad):
    dseq = u_ref.shape[0]
    upad[0:CONV_HIST] = hist_ref[...]
    upad[CONV_HIST:CONV_HIST + dseq] = u_ref[...]
    st_ref[...] = upad[dseq:dseq + CONV_HIST]
    for t in range(dseq):
        acc = jnp.broadcast_to(bdw_ref[...], upad.shape[1:])
        for j in range(CONV_WIDTH):
            acc = acc + wdw_ref[j:j + 1, :] * upad[t + j]
        z_ref[t] = _layernorm_silu(acc, lng_ref[...], lnb_ref[...])


def conv_mid_sample(u, hist, w_dw, b_dw, ln_g, ln_b):
    dseq, db, _ = u.shape
    whole = lambda shape: pl.BlockSpec(shape, lambda i: (0,) * len(shape))
    return pl.pallas_call(
        _conv_mid_sample_kernel,
        out_shape=(jax.ShapeDtypeStruct((dseq, db, D_MODEL), F32),
                   jax.ShapeDtypeStruct((CONV_HIST, db, D_MODEL), F32)),
        grid=(1,),
        in_specs=[whole(u.shape), whole(hist.shape), whole(w_dw.shape), whole(b_dw.shape), whole(ln_g.shape),
                  whole(ln_b.shape)],
        out_specs=(whole((dseq, db, D_MODEL)), whole((CONV_HIST, db, D_MODEL))),
        scratch_shapes=[pltpu.VMEM((CONV_HIST + dseq, db, D_MODEL), F32)],
        compiler_params=_params("arbitrary"),
        name="conv_mid_sample",
    )(u, hist, w_dw, b_dw, ln_g, ln_b)


def _mix_ffn_kernel(*refs, fc, has_bias, final):
    h_ref, z_ref, wo_ref = refs[:3]
    refs = refs[3:]
    if has_bias:
        bo_ref, refs = refs[0], refs[1:]
    g_ref, wgu_ref, wd_ref = refs[:3]
    refs = refs[3:]
    if final:
        gf_ref, refs = refs[0], refs[1:]
    out_ref = refs[0]

    h1 = h_ref[...] + _dot(z_ref[...].astype(BF16), wo_ref[...])
    if has_bias:
        h1 = h1 + bo_ref[...]
    xn = _rmsnorm(h1, g_ref[...]).astype(BF16)
    acc = h1
    for c in range(D_FF // fc):
        gate = _dot(xn, wgu_ref[:, c * fc:(c + 1) * fc])
        up = _dot(xn, wgu_ref[:, D_FF + c * fc:D_FF + (c + 1) * fc])
        a = (gate * _sigmoid(gate) * up).astype(BF16)
        acc = acc + _dot(a, wd_ref[c * fc:(c + 1) * fc, :])
    out_ref[...] = _rmsnorm(acc, gf_ref[...]) if final else acc


def mix_ffn(h, z, wo_bf, bo, g_ffn, wgu_bf, wd_bf, g_final, *, tm, fc):
    m = h.shape[0]
    has_bias = bo is not None
    final = g_final is not None
    row = pl.BlockSpec((tm, D_MODEL), lambda i: (i, 0))
    args = [h, z, wo_bf]
    specs = [row, row, _resident((D_MODEL, D_MODEL))]
    if has_bias:
        args.append(bo)
        specs.append(_resident((1, D_MODEL)))
    args += [g_ffn, wgu_bf, wd_bf]
    specs += [_resident((1, D_MODEL)), _resident((D_MODEL, 2 * D_FF)), _resident((D_FF, D_MODEL))]
    if final:
        args.append(g_final)
        specs.append(_resident((1, D_MODEL)))
    return pl.pallas_call(
        functools.partial(_mix_ffn_kernel, fc=fc, has_bias=has_bias, final=final),
        out_shape=jax.ShapeDtypeStruct((m, D_MODEL), F32),
        grid=(m // tm,),
        in_specs=specs,
        out_specs=row,
        compiler_params=_params("parallel"),
        name="mix_ffn",
    )(*args)


def _attn_proj_kernel(h_ref, g_ref, wq_ref, wkvt_ref, wft_ref, bf_ref, q_ref, k_ref, v_ref, lf_ref, *bf16_refs,
                      prompt):
    xn = _rmsnorm(h_ref[0], g_ref[...]).astype(BF16)
    q = (_dot(xn, wq_ref[...]) * (HEAD_DIM ** -0.5)).astype(BF16)
    kvt = _dot_nt(wkvt_ref[...], xn)
    lft = _log_sigmoid(_dot_nt(wft_ref[...], xn) + bf_ref[...])
    if prompt:
        for p in range(N_PAIRS):
            q_ref[0, p] = q[:, p * LANES:(p + 1) * LANES]
        k_ref[0] = kvt[:D_MODEL]
        v_ref[0] = kvt[D_MODEL:]
        lf_ref[0] = lft
        kb_ref, vb_ref = bf16_refs
        kb_ref[0] = kvt[:D_MODEL].astype(BF16)
        vb_ref[0] = kvt[D_MODEL:].astype(BF16)
    else:
        q_ref[0] = q
        kv = kvt.T
        k_ref[0] = kv[:, :D_MODEL]
        v_ref[0] = kv[:, D_MODEL:]
        lf_ref[0] = lft


def attn_proj(h, g, wq_bf, wkvt_bf, wft_bf, bf_col, *, tm, prompt):
    b, s, _ = h.shape
    row = pl.BlockSpec((1, tm, D_MODEL), lambda i, j: (i, j, 0))
    col = pl.BlockSpec((1, D_MODEL, tm), lambda i, j: (i, 0, j))
    lf_spec = pl.BlockSpec((1, N_HEADS, tm), lambda i, j: (i, 0, j))
    lf_shape = jax.ShapeDtypeStruct((b, N_HEADS, s), F32)
    if prompt:
        out_shape = [jax.ShapeDtypeStruct((b, N_PAIRS, s, LANES), BF16),
                     jax.ShapeDtypeStruct((b, D_MODEL, s), F32), jax.ShapeDtypeStruct((b, D_MODEL, s), F32),
                     lf_shape,
                     jax.ShapeDtypeStruct((b, D_MODEL, s), BF16), jax.ShapeDtypeStruct((b, D_MODEL, s), BF16)]
        out_specs = [pl.BlockSpec((1, N_PAIRS, tm, LANES), lambda i, j: (i, 0, j, 0)), col, col, lf_spec, col, col]
    else:
        out_shape = [jax.ShapeDtypeStruct((b, s, D_MODEL), BF16),
                     jax.ShapeDtypeStruct((b, s, D_MODEL), F32), jax.ShapeDtypeStruct((b, s, D_MODEL), F32),
                     lf_shape]
        out_specs = [row, row, row, lf_spec]
    return pl.pallas_call(
        functools.partial(_attn_proj_kernel, prompt=prompt),
        out_shape=tuple(out_shape),
        grid=(b, s // tm),
        in_specs=[row,
                  _resident((1, D_MODEL)),
                  _resident((D_MODEL, D_MODEL)),
                  _resident((2 * D_MODEL, D_MODEL)),
                  _resident((N_HEADS, D_MODEL)),
                  _resident((N_HEADS, 1))],
        out_specs=tuple(out_specs),
        compiler_params=_params("parallel", "parallel"),
        name="attn_proj",
    )(h, g, wq_bf, wkvt_bf, wft_bf, bf_col)


CUMSUM_CHUNK = 256


def _cumsum_kernel(x_ref, c_ref):
    s = x_ref.shape[2]
    upper = _upper_tri(CUMSUM_CHUNK)
    carry = jnp.zeros((N_HEADS, 1), F32)
    for j in range(s // CUMSUM_CHUNK):
        sl = slice(j * CUMSUM_CHUNK, (j + 1) * CUMSUM_CHUNK)
        local = _dot_exact(x_ref[0, :, sl], upper)
        c_ref[0, :, sl] = local + carry
        carry = carry + local[:, CUMSUM_CHUNK - 1:CUMSUM_CHUNK]


def cumsum_time(x):
    b, h, s = x.shape
    spec = pl.BlockSpec((1, h, s), lambda i: (i, 0, 0))
    return pl.pallas_call(
        _cumsum_kernel,
        out_shape=jax.ShapeDtypeStruct((b, h, s), F32),
        grid=(b,),
        in_specs=[spec],
        out_specs=spec,
        compiler_params=_params("parallel"),
        name="cumsum_time",
    )(x)


def _flash_kernel(qi_ref, ki_ref, q_ref, kt_ref, vt_ref, c_ref, o_ref, acc_ref, m_ref, l_ref):
    t = pl.program_id(1)
    qi = qi_ref[t]
    ki = ki_ref[t]

    @pl.when(ki == 0)
    def _():
        m_ref[...] = jnp.full(m_ref.shape, NEG, F32)
        l_ref[...] = jnp.zeros(l_ref.shape, F32)
        acc_ref[...] = jnp.zeros(acc_ref.shape, F32)

    lane = lax.broadcasted_iota(jnp.int32, (1, LANES), 1)

    def sweep(masked):
        def pair(p, carry):
            q2 = q_ref[0, p]
            kt2 = kt_ref[0, p]
            vt2 = vt_ref[0, p]
            c2 = c_ref[0, p]
            acc = acc_ref[p]
            for hh in range(2):
                sel = (lane < HEAD_DIM) if hh == 0 else (lane >= HEAD_DIM)
                qh = jnp.where(sel, q2, jnp.zeros_like(q2))
                s = _dot(qh, kt2) - c2[hh:hh + 1, :]
                if masked:
                    rows = lax.broadcasted_iota(jnp.int32, s.shape, 0)
                    cols = lax.broadcasted_iota(jnp.int32, s.shape, 1)
                    s = jnp.where(cols <= rows, s, NEG)
                idx = 2 * p + hh
                m_prev = m_ref[idx]
                m_new = jnp.maximum(m_prev, jnp.max(s, axis=1, keepdims=True))
                alpha = jnp.exp(m_prev - m_new)
                pr = jnp.exp(s - m_new)
                l_ref[idx] = alpha * l_ref[idx] + jnp.sum(pr, axis=1, keepdims=True)
                m_ref[idx] = m_new
                pv = _dot_nt(pr.astype(BF16), vt2)
                acc = jnp.where(sel, alpha * acc + pv, acc)
            acc_ref[p] = acc
            return carry

        lax.fori_loop(0, N_PAIRS, pair, 0)

    @pl.when(ki < qi)
    def _():
        sweep(False)

    @pl.when(ki == qi)
    def _():
        sweep(True)
        for p in range(N_PAIRS):
            inv = jnp.where(lane < HEAD_DIM, 1.0 / l_ref[2 * p], 1.0 / l_ref[2 * p + 1])
            o_ref[0, :, p * LANES:(p + 1) * LANES] = (acc_ref[p] * inv).astype(o_ref.dtype)


def flash_prompt(q8, kt8, vt8, c8, *, tq):
    b, _, s, _ = q8.shape
    nq = s // tq
    qi = jnp.asarray([i for i in range(nq) for _ in range(i + 1)], jnp.int32)
    ki = jnp.asarray([j for i in range(nq) for j in range(i + 1)], jnp.int32)
    grid_spec = pltpu.PrefetchScalarGridSpec(
        num_scalar_prefetch=2,
        grid=(b, int(qi.shape[0])),
        in_specs=[pl.BlockSpec((1, N_PAIRS, tq, LANES), lambda i, t, qi, ki: (i, 0, qi[t], 0)),
                  pl.BlockSpec((1, N_PAIRS, LANES, tq), lambda i, t, qi, ki: (i, 0, 0, ki[t])),
                  pl.BlockSpec((1, N_PAIRS, LANES, tq), lambda i, t, qi, ki: (i, 0, 0, ki[t])),
                  pl.BlockSpec((1, N_PAIRS, 2, tq), lambda i, t, qi, ki: (i, 0, 0, ki[t]))],
        out_specs=pl.BlockSpec((1, tq, D_MODEL), lambda i, t, qi, ki: (i, qi[t], 0)),
        scratch_shapes=[pltpu.VMEM((N_PAIRS, tq, LANES), F32),
                        pltpu.VMEM((N_HEADS, tq, 1), F32),
                        pltpu.VMEM((N_HEADS, tq, 1), F32)],
    )
    return pl.pallas_call(
        _flash_kernel,
        out_shape=jax.ShapeDtypeStruct((b, s, D_MODEL), BF16),
        grid_spec=grid_spec,
        compiler_params=_params("parallel", "arbitrary"),
        name="flash_prompt",
    )(qi, ki, q8, kt8, vt8, c8)


NEW_COLS = 16


def _sample_attn_kernel(pt_ref, q_ref, kn_ref, vn_ref, lfn_ref, *refs, group, dec_seq):
    kt_refs = refs[:group]
    vt_refs = refs[group:2 * group]
    lp_refs = refs[2 * group:3 * group]
    o_ref, qbd_ref, acc_ref, m_ref, l_ref, carry_ref = refs[3 * group:]
    g = pl.program_id(1)
    n_rows = dec_seq * N_HEADS

    lane_d = lax.broadcasted_iota(jnp.int32, (N_HEADS, D_MODEL), 1)
    head_d = lax.broadcasted_iota(jnp.int32, (N_HEADS, D_MODEL), 0)
    head_mask = (lane_d // HEAD_DIM == head_d).astype(F32)

    @pl.when(g == 0)
    def _():
        qf = q_ref[0].astype(F32)
        for t in range(dec_seq):
            qbd_ref[t * N_HEADS:(t + 1) * N_HEADS, :] = (head_mask * qf[t:t + 1, :]).astype(BF16)
        m_ref[...] = jnp.full(m_ref.shape, NEG, F32)
        l_ref[...] = jnp.zeros(l_ref.shape, F32)
        acc_ref[...] = jnp.zeros(acc_ref.shape, F32)
        carry_ref[...] = jnp.zeros(carry_ref.shape, F32)

    def process(s, lf_rows, mask, pv):
        n = lf_rows.shape[1]
        lf_exp = jnp.concatenate([lf_rows] * dec_seq, axis=0)
        c = _dot_exact(lf_exp, _upper_tri(n)) + carry_ref[...]
        carry_ref[...] = c[:, n - 1:n]
        s = s - c
        if mask is not None:
            s = jnp.where(mask, s, NEG)
        m_prev = m_ref[...]
        m_new = jnp.maximum(m_prev, jnp.max(s, axis=1, keepdims=True))
        alpha = jnp.exp(m_prev - m_new)
        pr = jnp.exp(s - m_new)
        l_ref[...] = alpha * l_ref[...] + jnp.sum(pr, axis=1, keepdims=True)
        m_ref[...] = m_new
        acc_ref[...] = alpha * acc_ref[...] + pv(pr.astype(BF16))

    for i in range(group):
        vt = vt_refs[i]
        process(_dot(qbd_ref[...], kt_refs[i][...].astype(BF16)), lp_refs[i][...], None,
                lambda pr, vt=vt: _dot_nt(pr, vt[...].astype(BF16)))

    @pl.when(g == pl.num_programs(1) - 1)
    def _():
        r = lax.broadcasted_iota(jnp.int32, (n_rows, NEW_COLS), 0)
        col = lax.broadcasted_iota(jnp.int32, (n_rows, NEW_COLS), 1)
        process(_dot_nt(qbd_ref[...], kn_ref[0].astype(BF16)), lfn_ref[0], col <= r // N_HEADS,
                lambda pr: _dot(pr, vn_ref[0].astype(BF16)))
        o = acc_ref[...] / l_ref[...]
        rows = [jnp.sum(o[t * N_HEADS:(t + 1) * N_HEADS, :] * head_mask, axis=0, keepdims=True)
                for t in range(dec_seq)]
        o_ref[0] = jnp.concatenate(rows, axis=0)


def sample_attention(page_ids, q, k_new, v_new, lft_new, cache_kt, cache_vt, cache_lft, *, group):
    db, n_pages = page_ids.shape
    dec_seq = q.shape[1]
    n_rows = dec_seq * N_HEADS

    def page_spec(rows, i):
        return pl.BlockSpec((None, rows, PAGE), lambda b, g, pt: (pt[b, g * group + i], 0, 0))

    def per_batch(rows, width):
        return pl.BlockSpec((1, rows, width), lambda b, g, pt: (b, 0, 0))

    grid_spec = pltpu.PrefetchScalarGridSpec(
        num_scalar_prefetch=1,
        grid=(db, n_pages // group),
        in_specs=([per_batch(dec_seq, D_MODEL), per_batch(NEW_COLS, D_MODEL), per_batch(NEW_COLS, D_MODEL),
                   per_batch(N_HEADS, NEW_COLS)]
                  + [page_spec(D_MODEL, i) for i in range(group)]
                  + [page_spec(D_MODEL, i) for i in range(group)]
                  + [page_spec(N_HEADS, i) for i in range(group)]),
        out_specs=per_batch(dec_seq, D_MODEL),
        scratch_shapes=[pltpu.VMEM((n_rows, D_MODEL), BF16),
                        pltpu.VMEM((n_rows, D_MODEL), F32),
                        pltpu.VMEM((n_rows, 1), F32),
                        pltpu.VMEM((n_rows, 1), F32),
                        pltpu.VMEM((n_rows, 1), F32)],
    )
    return pl.pallas_call(
        functools.partial(_sample_attn_kernel, group=group, dec_seq=dec_seq),
        out_shape=jax.ShapeDtypeStruct((db, dec_seq, D_MODEL), F32),
        grid_spec=grid_spec,
        compiler_params=_params("parallel", "arbitrary"),
        name="sample_attention",
    )(page_ids, q, k_new, v_new, lft_new, *([cache_kt] * group), *([cache_vt] * group), *([cache_lft] * group))


PROMPT_TM = 512
CONV_TT = 128
FFN_CHUNK = 1408
ATTN_TQ = 512
PAGE_GROUP = 4


def kernel(x_prompt, x_sample, state_conv, cache_k, cache_v, cache_logf, page_table, norm_mix_g, norm_ffn_g,
           norm_final_g, conv_w_in, conv_b_in, conv_w_dw, conv_b_dw, conv_ln_g, conv_ln_b, conv_w_out, conv_b_out,
           attn_w_in, attn_b_f, attn_w_out, ffn_w_gu, ffn_w_down):
    bsz, seq, d = x_prompt.shape
    dbsz, dseq, _ = x_sample.shape
    n_sample = dbsz * dseq
    row = lambda a: a.reshape(1, -1)
    to_time_major = lambda a: jnp.transpose(a.reshape(dbsz, dseq, -1), (1, 0, 2)).reshape(n_sample, -1)
    to_batch_major = lambda a: jnp.transpose(a.reshape(dseq, dbsz, -1), (1, 0, 2))

    w_in = conv_w_in[0].astype(BF16)
    w_out = conv_w_out[0].astype(BF16)
    wgu0 = ffn_w_gu[0].astype(BF16)
    wd0 = ffn_w_down[0].astype(BF16)
    conv_args = (conv_w_dw[0], row(conv_b_dw[0]), row(conv_ln_g[0]), row(conv_ln_b[0]))

    hp = x_prompt.reshape(bsz * seq, d)
    hs = to_time_major(x_sample)

    up = conv_in(hp, row(norm_mix_g[0]), w_in, row(conv_b_in[0]), tm=PROMPT_TM)
    us = conv_in(hs, row(norm_mix_g[0]), w_in, row(conv_b_in[0]), tm=n_sample)
    zp, conv_state_p = conv_mid(up.reshape(bsz, seq, d), *conv_args, tt=CONV_TT)
    zs, conv_state_s = conv_mid_sample(us.reshape(dseq, dbsz, d), jnp.transpose(state_conv[0], (1, 0, 2)),
                                       *conv_args)
    hp = mix_ffn(hp, zp.reshape(bsz * seq, d), w_out, row(conv_b_out[0]), row(norm_ffn_g[0]), wgu0, wd0, None,
                 tm=PROMPT_TM, fc=FFN_CHUNK)
    hs = mix_ffn(hs, zs.reshape(n_sample, d), w_out, row(conv_b_out[0]), row(norm_ffn_g[0]), wgu0, wd0, None,
                 tm=n_sample, fc=FFN_CHUNK)

    w_in_t = jnp.transpose(attn_w_in[0])
    wq = attn_w_in[0, :, :d].astype(BF16)
    wkvt = w_in_t[d:3 * d].astype(BF16)
    wft = w_in_t[3 * d:].astype(BF16)
    bf_col = attn_b_f[0].reshape(N_HEADS, 1)
    wo1 = attn_w_out[0].astype(BF16)
    wgu1 = ffn_w_gu[1].astype(BF16)
    wd1 = ffn_w_down[1].astype(BF16)
    proj_args = (row(norm_mix_g[1]), wq, wkvt, wft, bf_col)

    q8, ktp, vtp, lftp, ktb, vtb = attn_proj(hp.reshape(bsz, seq, d), *proj_args, tm=PROMPT_TM, prompt=True)
    c = cumsum_time(lftp)
    pairs = lambda a: a.reshape(bsz, N_PAIRS, LANES, seq)
    op = flash_prompt(q8, pairs(ktb), pairs(vtb), c.reshape(bsz, N_PAIRS, 2, seq), tq=ATTN_TQ)
    hp = mix_ffn(hp, op.reshape(bsz * seq, d), wo1, None, row(norm_ffn_g[1]), wgu1, wd1, row(norm_final_g),
                 tm=PROMPT_TM, fc=FFN_CHUNK)

    qs, ks, vs, lfts = attn_proj(hs.reshape(1, n_sample, d), *proj_args, tm=n_sample, prompt=False)
    ks_b = to_batch_major(ks[0])
    vs_b = to_batch_major(vs[0])
    lfts_b = jnp.transpose(lfts.reshape(N_HEADS, dseq, dbsz), (2, 0, 1))
    pad_rows = lambda a: jnp.pad(a, ((0, 0), (0, NEW_COLS - dseq), (0, 0)))
    cache_kt = jnp.transpose(cache_k, (0, 1, 3, 4, 2)).reshape(-1, d, PAGE)
    cache_vt = jnp.transpose(cache_v, (0, 1, 3, 4, 2)).reshape(-1, d, PAGE)
    cache_lft = jnp.transpose(cache_logf, (0, 1, 3, 2)).reshape(-1, N_HEADS, PAGE)
    os_ = sample_attention(page_table, to_batch_major(qs[0]), pad_rows(ks_b), pad_rows(vs_b),
                           jnp.pad(lfts_b, ((0, 0), (0, 0), (0, NEW_COLS - dseq))),
                           cache_kt, cache_vt, cache_lft, group=PAGE_GROUP)
    hs = mix_ffn(hs, to_time_major(os_), wo1, None, row(norm_ffn_g[1]), wgu1, wd1, row(norm_final_g),
                 tm=n_sample, fc=FFN_CHUNK)

    heads_t = lambda a: jnp.transpose(a.reshape(bsz, N_HEADS, HEAD_DIM, seq), (0, 3, 1, 2))[None]
    heads = lambda a: a.reshape(1, dbsz, dseq, N_HEADS, HEAD_DIM)
    return (hp.reshape(bsz, seq, d), to_batch_major(hs),
            conv_state_p[None], jnp.transpose(conv_state_s, (1, 0, 2))[None],
            heads_t(ktp), heads_t(vtp), jnp.transpose(lftp, (0, 2, 1))[None],
            heads(ks_b), heads(vs_b), jnp.transpose(lfts_b, (0, 2, 1))[None])
```

```python
import functools

import jax
import jax.numpy as jnp
from jax import lax
from jax.experimental import pallas as pl
from jax.experimental.pallas import tpu as pltpu

D_MODEL = 1024
N_HEADS = 16
HEAD_DIM = 64
N_PAIRS = N_HEADS // 2
LANES = 128
D_FF = 2816
CONV_WIDTH = 31
CONV_HIST = CONV_WIDTH - 1
HIST_PAD = 32
EPS = 1e-6
PAGE = 128
NEG = -1e30
VMEM_LIMIT_BYTES = 56 * 1024 * 1024

F32 = jnp.float32
BF16 = jnp.bfloat16
NT_DIMS = (((1,), (1,)), ((), ()))


def _params(*sem):
    return pltpu.CompilerParams(dimension_semantics=sem, vmem_limit_bytes=VMEM_LIMIT_BYTES)


def _resident(shape):
    nd = len(shape)
    return pl.BlockSpec(shape, lambda *_: (0,) * nd, pipeline_mode=pl.Buffered(1))


def _rmsnorm(x, g):
    return x * lax.rsqrt(jnp.mean(x * x, axis=-1, keepdims=True) + EPS) * g


def _sigmoid(x):
    return 1.0 / (1.0 + jnp.exp(-x))


def _log_sigmoid(x):
    return jnp.minimum(x, 0.0) - jnp.log1p(jnp.exp(-jnp.abs(x)))


def _layernorm_silu(y, g, b):
    mu = jnp.mean(y, axis=-1, keepdims=True)
    yc = y - mu
    var = jnp.mean(yc * yc, axis=-1, keepdims=True)
    yn = yc * lax.rsqrt(var + EPS) * g + b
    return yn * _sigmoid(yn)


def _dot(a, b):
    return jnp.dot(a, b, preferred_element_type=F32)


def _dot_nt(a, b):
    return lax.dot_general(a, b, NT_DIMS, preferred_element_type=F32)


def _dot_exact(a, b):
    return jnp.dot(a, b, preferred_element_type=F32, precision=lax.Precision.HIGHEST)


def _upper_tri(n):
    return (lax.broadcasted_iota(jnp.int32, (n, n), 0) <= lax.broadcasted_iota(jnp.int32, (n, n), 1)).astype(F32)


def _conv_in_kernel(h_ref, g_ref, w_ref, b_ref, u_ref):
    xn = _rmsnorm(h_ref[...], g_ref[...]).astype(BF16)
    hh = _dot(xn, w_ref[...]) + b_ref[...]
    u_ref[...] = hh[:, :D_MODEL] * _sigmoid(hh[:, D_MODEL:])


def conv_in(h, g, w_bf, b, *, tm):
    m = h.shape[0]
    return pl.pallas_call(
        _conv_in_kernel,
        out_shape=jax.ShapeDtypeStruct((m, D_MODEL), F32),
        grid=(m // tm,),
        in_specs=[pl.BlockSpec((tm, D_MODEL), lambda i: (i, 0)),
                  _resident((1, D_MODEL)),
                  _resident((D_MODEL, 2 * D_MODEL)),
                  _resident((1, 2 * D_MODEL))],
        out_specs=pl.BlockSpec((tm, D_MODEL), lambda i: (i, 0)),
        compiler_params=_params("parallel"),
        name="conv_in",
    )(h, g, w_bf, b)


SUBLANES = 8
CONV_RC = 16
CONV_PARTS = 1


def _conv_mid_kernel(u_ref, wdw_ref, bdw_ref, lng_ref, lnb_ref, z_ref, st_ref, ubuf, ybuf, *, tt):
    t = pl.program_id(1)
    n_shift = ubuf.shape[1]

    @pl.when(t == 0)
    def _():
        ubuf[0, 0:HIST_PAD, :] = jnp.zeros((HIST_PAD, D_MODEL), F32)

    @pl.when(t > 0)
    def _():
        ubuf[0, 0:HIST_PAD, :] = ubuf[0, tt:tt + HIST_PAD, :]

    ubuf[0, HIST_PAD:HIST_PAD + tt, :] = u_ref[0]
    st_ref[0] = ubuf[0, tt + HIST_PAD - CONV_HIST:tt + HIST_PAD, :]
    for s in range(1, SUBLANES):
        ubuf[s, 0:n_shift - SUBLANES, :] = ubuf[0, s:s + n_shift - SUBLANES, :]

    off = HIST_PAD - CONV_HIST
    rc = min(tt, CONV_RC)

    def row_chunk(i, carry):
        r0 = pl.multiple_of(i * rc, rc)
        for l0 in range(0, D_MODEL, LANES):
            parts = [jnp.broadcast_to(bdw_ref[:, l0:l0 + LANES], (rc, LANES))] + [None] * (CONV_PARTS - 1)
            for j in range(CONV_WIDTH):
                shift, base = (j + off) % SUBLANES, (j + off) // SUBLANES * SUBLANES
                term = wdw_ref[j:j + 1, l0:l0 + LANES] * ubuf[shift, pl.ds(r0 + base, rc), l0:l0 + LANES]
                k = j % CONV_PARTS
                parts[k] = term if parts[k] is None else parts[k] + term
            acc = parts[0]
            for part in parts[1:]:
                acc = acc + part
            ybuf[pl.ds(r0, rc), l0:l0 + LANES] = acc
        return carry

    lax.fori_loop(0, tt // rc, row_chunk, 0)

    z_ref[0] = _layernorm_silu(ybuf[...], lng_ref[...], lnb_ref[...]).astype(z_ref.dtype)


def conv_mid(u, w_dw, b_dw, ln_g, ln_b, *, tt):
    b, t, _ = u.shape
    return pl.pallas_call(
        functools.partial(_conv_mid_kernel, tt=tt),
        out_shape=(jax.ShapeDtypeStruct((b, t, D_MODEL), BF16),
                   jax.ShapeDtypeStruct((b, CONV_HIST, D_MODEL), F32)),
        grid=(b, t // tt),
        in_specs=[pl.BlockSpec((1, tt, D_MODEL), lambda i, j: (i, j, 0)),
                  _resident((CONV_WIDTH, D_MODEL)),
                  _resident((1, D_MODEL)),
                  _resident((1, D_MODEL)),
                  _resident((1, D_MODEL))],
        out_specs=(pl.BlockSpec((1, tt, D_MODEL), lambda i, j: (i, j, 0)),
                   pl.BlockSpec((1, CONV_HIST, D_MODEL), lambda i, j: (i, 0, 0))),
        scratch_shapes=[pltpu.VMEM((SUBLANES, HIST_PAD + tt, D_MODEL), F32),
                        pltpu.VMEM((tt, D_MODEL), F32)],
        compiler_params=_params("parallel", "arbitrary"),
        name="conv_mid",
    )(u, w_dw, b_dw, ln_g, ln_b)


def _conv_mid_sample_kernel(u_ref, hist_ref, wdw_ref, bdw_ref, lng_ref, lnb_ref, z_ref, st_ref, upad):
    dseq = u_ref.shape[0]
    upad[0:CONV_HIST] = hist_ref[...]
    upad[CONV_HIST:CONV_HIST + dseq] = u_ref[...]
    st_ref[...] = upad[dseq:dseq + CONV_HIST]
    for t in range(dseq):
        acc = jnp.broadcast_to(bdw_ref[...], upad.shape[1:])
        for j in range(CONV_WIDTH):
            acc = acc + wdw_ref[j:j + 1, :] * upad[t + j]
        z_ref[t] = _layernorm_silu(acc, lng_ref[...], lnb_ref[...])


def conv_mid_sample(u, hist, w_dw, b_dw, ln_g, ln_b):
    dseq, db, _ = u.shape
    whole = lambda shape: pl.BlockSpec(shape, lambda i: (0,) * len(shape))
    return pl.pallas_call(
        _conv_mid_sample_kernel,
        out_shape=(jax.ShapeDtypeStruct((dseq, db, D_MODEL), F32),
                   jax.ShapeDtypeStruct((CONV_HIST, db, D_MODEL), F32)),
        grid=(1,),
        in_specs=[whole(u.shape), whole(hist.shape), whole(w_dw.shape), whole(b_dw.shape), whole(ln_g.shape),
                  whole(ln_b.shape)],
        out_specs=(whole((dseq, db, D_MODEL)), whole((CONV_HIST, db, D_MODEL))),
        scratch_shapes=[pltpu.VMEM((CONV_HIST + dseq, db, D_MODEL), F32)],
        compiler_params=_params("arbitrary"),
        name="conv_mid_sample",
    )(u, hist, w_dw, b_dw, ln_g, ln_b)


def _mix_ffn_kernel(*refs, fc, has_bias, final):
    h_ref, z_ref, wo_ref = refs[:3]
    refs = refs[3:]
    if has_bias:
        bo_ref, refs = refs[0], refs[1:]
    g_ref, wgu_ref, wd_ref = refs[:3]
    refs = refs[3:]
    if final:
        gf_ref, refs = refs[0], refs[1:]
    out_ref = refs[0]

    h1 = h_ref[...] + _dot(z_ref[...].astype(BF16), wo_ref[...])
    if has_bias:
        h1 = h1 + bo_ref[...]
    xn = _rmsnorm(h1, g_ref[...]).astype(BF16)
    acc = h1
    for c in range(D_FF // fc):
        gate = _dot(xn, wgu_ref[:, c * fc:(c + 1) * fc])
        up = _dot(xn, wgu_ref[:, D_FF + c * fc:D_FF + (c + 1) * fc])
        a = (gate * _sigmoid(gate) * up).astype(BF16)
        acc = acc + _dot(a, wd_ref[c * fc:(c + 1) * fc, :])
    out_ref[...] = _rmsnorm(acc, gf_ref[...]) if final else acc


def mix_ffn(h, z, wo_bf, bo, g_ffn, wgu_bf, wd_bf, g_final, *, tm, fc):
    m = h.shape[0]
    has_bias = bo is not None
    final = g_final is not None
    row = pl.BlockSpec((tm, D_MODEL), lambda i: (i, 0))
    args = [h, z, wo_bf]
    specs = [row, row, _resident((D_MODEL, D_MODEL))]
    if has_bias:
        args.append(bo)
        specs.append(_resident((1, D_MODEL)))
    args += [g_ffn, wgu_bf, wd_bf]
    specs += [_resident((1, D_MODEL)), _resident((D_MODEL, 2 * D_FF)), _resident((D_FF, D_MODEL))]
    if final:
        args.append(g_final)
        specs.append(_resident((1, D_MODEL)))
    return pl.pallas_call(
        functools.partial(_mix_ffn_kernel, fc=fc, has_bias=has_bias, final=final),
        out_shape=jax.ShapeDtypeStruct((m, D_MODEL), F32),
        grid=(m // tm,),
        in_specs=specs,
        out_specs=row,
        compiler_params=_params("parallel"),
        name="mix_ffn",
    )(*args)


C_ROW = HEAD_DIM
C_TERMS = 3
C_ROWS = 16
L_LANE = HEAD_DIM
LOG2E = 1.4426950408889634


def _attn_proj_kernel(h_ref, g_ref, wq_ref, wkvt_ref, wft_ref, bf_ref, q_ref, k_ref, v_ref, lf_ref, *slab_refs,
                      prompt):
    xn = _rmsnorm(h_ref[0], g_ref[...]).astype(BF16)
    q = _dot(xn, wq_ref[...])
    kvt = _dot_nt(wkvt_ref[...], xn)
    lf_ref[0] = _log_sigmoid(_dot_nt(wft_ref[...], xn) + bf_ref[...])
    if prompt:
        tm = q.shape[0]
        k_ref[0] = kvt[:D_MODEL]
        v_ref[0] = kvt[D_MODEL:]
        ka_ref, va_ref = slab_refs
        lane = lax.broadcasted_iota(jnp.int32, (1, LANES), 1)
        q_tail = jnp.where((lane >= C_ROW) & (lane < C_ROW + C_TERMS), -1.0, 0.0)
        row = lax.broadcasted_iota(jnp.int32, (HEAD_DIM, 1), 0)
        v_tail = jnp.broadcast_to(jnp.where(row == 0, 1.0, 0.0), (HEAD_DIM, tm)).astype(BF16)
        k_tail = jnp.zeros((HEAD_DIM, tm), BF16)
        q = q * (HEAD_DIM ** -0.5 * LOG2E)
        for p in range(N_PAIRS):
            blk = q[:, p * LANES:(p + 1) * LANES]
            q_ref[0, 2 * p] = jnp.where(lane < HEAD_DIM, blk, q_tail).astype(BF16)
            q_ref[0, 2 * p + 1] = jnp.where(lane < HEAD_DIM, pltpu.roll(blk, HEAD_DIM, 1), q_tail).astype(BF16)
        for h in range(N_HEADS):
            ka_ref[0, h, :HEAD_DIM] = kvt[h * HEAD_DIM:(h + 1) * HEAD_DIM].astype(BF16)
            ka_ref[0, h, HEAD_DIM:] = k_tail
            va_ref[0, h, :HEAD_DIM] = kvt[D_MODEL + h * HEAD_DIM:D_MODEL + (h + 1) * HEAD_DIM].astype(BF16)
            va_ref[0, h, HEAD_DIM:] = v_tail
    else:
        q_ref[0] = (q * (HEAD_DIM ** -0.5)).astype(BF16)
        kv = kvt.T
        k_ref[0] = kv[:, :D_MODEL]
        v_ref[0] = kv[:, D_MODEL:]


def attn_proj(h, g, wq_bf, wkvt_bf, wft_bf, bf_col, *, tm, prompt):
    b, s, _ = h.shape
    row = pl.BlockSpec((1, tm, D_MODEL), lambda i, j: (i, j, 0))
    col = pl.BlockSpec((1, D_MODEL, tm), lambda i, j: (i, 0, j))
    lf_spec = pl.BlockSpec((1, N_HEADS, tm), lambda i, j: (i, 0, j))
    lf_shape = jax.ShapeDtypeStruct((b, N_HEADS, s), F32)
    if prompt:
        slab = pl.BlockSpec((1, N_HEADS, LANES, tm), lambda i, j: (i, 0, 0, j))
        out_shape = [jax.ShapeDtypeStruct((b, N_HEADS, s, LANES), BF16),
                     jax.ShapeDtypeStruct((b, D_MODEL, s), F32), jax.ShapeDtypeStruct((b, D_MODEL, s), F32),
                     lf_shape,
                     jax.ShapeDtypeStruct((b, N_HEADS, LANES, s), BF16),
                     jax.ShapeDtypeStruct((b, N_HEADS, LANES, s), BF16)]
        out_specs = [pl.BlockSpec((1, N_HEADS, tm, LANES), lambda i, j: (i, 0, j, 0)), col, col, lf_spec, slab, slab]
    else:
        out_shape = [jax.ShapeDtypeStruct((b, s, D_MODEL), BF16),
                     jax.ShapeDtypeStruct((b, s, D_MODEL), F32), jax.ShapeDtypeStruct((b, s, D_MODEL), F32),
                     lf_shape]
        out_specs = [row, row, row, lf_spec]
    return pl.pallas_call(
        functools.partial(_attn_proj_kernel, prompt=prompt),
        out_shape=tuple(out_shape),
        grid=(b, s // tm),
        in_specs=[row,
                  _resident((1, D_MODEL)),
                  _resident((D_MODEL, D_MODEL)),
                  _resident((2 * D_MODEL, D_MODEL)),
                  _resident((N_HEADS, D_MODEL)),
                  _resident((N_HEADS, 1))],
        out_specs=tuple(out_specs),
        compiler_params=_params("parallel", "parallel"),
        name="attn_proj",
    )(h, g, wq_bf, wkvt_bf, wft_bf, bf_col)


CUMSUM_CHUNK = 256


def _cumsum_kernel(x_ref, slab_hbm_ref, c_ref):
    del slab_hbm_ref
    s = x_ref.shape[2]
    upper = _upper_tri(CUMSUM_CHUNK)
    row = lax.broadcasted_iota(jnp.int32, (C_ROWS, CUMSUM_CHUNK), 0)
    carry = jnp.zeros((N_HEADS, 1), F32)
    for j in range(s // CUMSUM_CHUNK):
        sl = slice(j * CUMSUM_CHUNK, (j + 1) * CUMSUM_CHUNK)
        local = _dot_exact(x_ref[0, :, sl], upper)
        c = (local + carry) * LOG2E
        carry = carry + local[:, CUMSUM_CHUNK - 1:CUMSUM_CHUNK]
        hi = c.astype(BF16).astype(F32)
        mid = (c - hi).astype(BF16).astype(F32)
        lo = c - hi - mid
        for h in range(N_HEADS):
            rows = jnp.where(row == 0, hi[h:h + 1], jnp.where(row == 1, mid[h:h + 1],
                                                               jnp.where(row == 2, lo[h:h + 1], 0.0)))
            c_ref[0, h, :, sl] = rows.astype(BF16)


def cumsum_into_slabs(lft, k_slabs):
    b, h, s = lft.shape
    return pl.pallas_call(
        _cumsum_kernel,
        out_shape=jax.ShapeDtypeStruct(k_slabs.shape, k_slabs.dtype),
        grid=(b,),
        in_specs=[pl.BlockSpec((1, h, s), lambda i: (i, 0, 0)),
                  pl.BlockSpec(memory_space=pl.ANY)],
        out_specs=pl.BlockSpec((1, h, C_ROWS, s), lambda i: (i, 0, C_ROW // C_ROWS, 0)),
        input_output_aliases={1: 0},
        compiler_params=_params("parallel"),
        name="cumsum_into_slabs",
    )(lft, k_slabs)


def _flash_kernel(qi_ref, ki_ref, q_ref, k_ref, v_ref, o_ref, acc_ref, m_ref, *, tq, rb, hpi):
    t = pl.program_id(1)
    qi = qi_ref[t]
    ki = ki_ref[t]

    @pl.when(ki == 0)
    def _():
        m_ref[...] = jnp.full(m_ref.shape, NEG, F32)
        acc_ref[...] = jnp.zeros(acc_ref.shape, F32)

    def sweep(diagonal):
        def heads(j, carry):
            chains = [(j * hpi + hh, r) for hh in range(hpi) for r in range(tq // rb)]
            rows = [slice(r * rb, (r + 1) * rb) for _, r in chains]
            nk = [(r + 1) * rb if diagonal else tq for _, r in chains]
            n = range(len(chains))
            m_prev = [m_ref[chains[i][0], rows[i], :] for i in n]
            acc = [acc_ref[chains[i][0], rows[i], :] for i in n]
            s = [_dot(q_ref[0, chains[i][0], rows[i], :], k_ref[0, chains[i][0], :, :nk[i]]) for i in n]
            m_new, pv = [], []
            for i in n:
                si = s[i]
                if diagonal:
                    qpos = lax.broadcasted_iota(jnp.int32, si.shape, 0) + chains[i][1] * rb
                    kpos = lax.broadcasted_iota(jnp.int32, si.shape, 1)
                    si = jnp.where(kpos <= qpos, si, NEG)
                m_new.append(jnp.maximum(m_prev[i], jnp.max(si, axis=1, keepdims=True)))
                pr = jnp.exp2(si - jnp.concatenate([m_new[i]] * (nk[i] // LANES), axis=1))
                pv.append(_dot_nt(pr.astype(BF16), v_ref[0, chains[i][0], :, :nk[i]]))
            for i in n:
                acc_ref[chains[i][0], rows[i], :] = jnp.exp2(m_prev[i] - m_new[i]) * acc[i] + pv[i]
                m_ref[chains[i][0], rows[i], :] = m_new[i]
            return carry

        lax.fori_loop(0, N_HEADS // hpi, heads, 0)

    @pl.when(ki < qi)
    def _():
        sweep(False)

    @pl.when(ki == qi)
    def _():
        sweep(True)
        lane = lax.broadcasted_iota(jnp.int32, (1, LANES), 1)
        for p in range(N_PAIRS):
            even = acc_ref[2 * p]
            odd = acc_ref[2 * p + 1]
            even = even / even[:, L_LANE:L_LANE + 1]
            odd = odd / odd[:, L_LANE:L_LANE + 1]
            o_ref[0, :, p * LANES:(p + 1) * LANES] = jnp.where(
                lane < HEAD_DIM, even, pltpu.roll(odd, HEAD_DIM, 1)).astype(o_ref.dtype)


def flash_prompt(q_slabs, k_slabs, v_slabs, *, tq, rb, hpi):
    b, _, s, _ = q_slabs.shape
    nq = s // tq
    qi = jnp.asarray([i for i in range(nq) for _ in range(i + 1)], jnp.int32)
    ki = jnp.asarray([j for i in range(nq) for j in range(i + 1)], jnp.int32)
    grid_spec = pltpu.PrefetchScalarGridSpec(
        num_scalar_prefetch=2,
        grid=(b, int(qi.shape[0])),
        in_specs=[pl.BlockSpec((1, N_HEADS, tq, LANES), lambda i, t, qi, ki: (i, 0, qi[t], 0)),
                  pl.BlockSpec((1, N_HEADS, LANES, tq), lambda i, t, qi, ki: (i, 0, 0, ki[t])),
                  pl.BlockSpec((1, N_HEADS, LANES, tq), lambda i, t, qi, ki: (i, 0, 0, ki[t]))],
        out_specs=pl.BlockSpec((1, tq, D_MODEL), lambda i, t, qi, ki: (i, qi[t], 0)),
        scratch_shapes=[pltpu.VMEM((N_HEADS, tq, LANES), F32),
                        pltpu.VMEM((N_HEADS, tq, LANES), F32)],
    )
    return pl.pallas_call(
        functools.partial(_flash_kernel, tq=tq, rb=rb, hpi=hpi),
        out_shape=jax.ShapeDtypeStruct((b, s, D_MODEL), BF16),
        grid_spec=grid_spec,
        compiler_params=_params("parallel", "arbitrary"),
        name="flash_prompt",
    )(qi, ki, q_slabs, k_slabs, v_slabs)


NEW_COLS = 16


def _sample_attn_kernel(pt_ref, q_ref, kn_ref, vn_ref, lfn_ref, *refs, group, dec_seq):
    kt_refs = refs[:group]
    vt_refs = refs[group:2 * group]
    lp_refs = refs[2 * group:3 * group]
    o_ref, qbd_ref, acc_ref, m_ref, l_ref, carry_ref = refs[3 * group:]
    g = pl.program_id(1)
    n_rows = dec_seq * N_HEADS

    lane_d = lax.broadcasted_iota(jnp.int32, (N_HEADS, D_MODEL), 1)
    head_d = lax.broadcasted_iota(jnp.int32, (N_HEADS, D_MODEL), 0)
    head_mask = (lane_d // HEAD_DIM == head_d).astype(F32)

    @pl.when(g == 0)
    def _():
        qf = q_ref[0].astype(F32)
        for t in range(dec_seq):
            qbd_ref[t * N_HEADS:(t + 1) * N_HEADS, :] = (head_mask * qf[t:t + 1, :]).astype(BF16)
        m_ref[...] = jnp.full(m_ref.shape, NEG, F32)
        l_ref[...] = jnp.zeros(l_ref.shape, F32)
        acc_ref[...] = jnp.zeros(acc_ref.shape, F32)
        carry_ref[...] = jnp.zeros(carry_ref.shape, F32)

    def attend(scores, lf_rows, mask, pvs):
        n = lf_rows[0].shape[1]
        upper = _upper_tri(n)
        carry = carry_ref[...]
        m_prev = m_ref[...]
        m_new = m_prev
        s = []
        for sc, lf in zip(scores, lf_rows):
            local = _dot_exact(jnp.concatenate([lf] * dec_seq, axis=0), upper)
            si = sc - (local + carry[:, :n])
            carry = carry + local[:, n - 1:n]
            if mask is not None:
                si = jnp.where(mask, si, NEG)
            m_new = jnp.maximum(m_new, jnp.max(si, axis=1, keepdims=True))
            s.append(si)
        alpha = jnp.exp(m_prev - m_new)
        l_new = alpha * l_ref[...]
        acc = alpha[:, :1] * acc_ref[...]
        for si, pv in zip(s, pvs):
            pr = jnp.exp(si - m_new[:, :n])
            l_new = l_new + jnp.sum(pr, axis=1, keepdims=True)
            acc = acc + pv(pr.astype(BF16))
        carry_ref[...] = carry
        m_ref[...] = m_new
        l_ref[...] = l_new
        acc_ref[...] = acc

    qbd = qbd_ref[...]
    attend([_dot(qbd, kt[...].astype(BF16)) for kt in kt_refs], [lp[...] for lp in lp_refs], None,
           [lambda pr, vt=vt: _dot_nt(pr, vt[...].astype(BF16)) for vt in vt_refs])

    @pl.when(g == pl.num_programs(1) - 1)
    def _():
        r = lax.broadcasted_iota(jnp.int32, (n_rows, NEW_COLS), 0)
        col = lax.broadcasted_iota(jnp.int32, (n_rows, NEW_COLS), 1)
        attend([_dot_nt(qbd_ref[...], kn_ref[0].astype(BF16))], [lfn_ref[0]], col <= r // N_HEADS,
               [lambda pr: _dot(pr, vn_ref[0].astype(BF16))])
        o = acc_ref[...] / l_ref[:, :1]
        rows = [jnp.sum(o[t * N_HEADS:(t + 1) * N_HEADS, :] * head_mask, axis=0, keepdims=True)
                for t in range(dec_seq)]
        o_ref[0] = jnp.concatenate(rows, axis=0)


def sample_attention(page_ids, q, k_new, v_new, lft_new, cache_kt, cache_vt, cache_lft, *, group):
    db, n_pages = page_ids.shape
    dec_seq = q.shape[1]
    n_rows = dec_seq * N_HEADS

    def page_spec(rows, i):
        return pl.BlockSpec((None, rows, PAGE), lambda b, g, pt: (pt[b, g * group + i], 0, 0))

    def per_batch(rows, width):
        return pl.BlockSpec((1, rows, width), lambda b, g, pt: (b, 0, 0))

    grid_spec = pltpu.PrefetchScalarGridSpec(
        num_scalar_prefetch=1,
        grid=(db, n_pages // group),
        in_specs=([per_batch(dec_seq, D_MODEL), per_batch(NEW_COLS, D_MODEL), per_batch(NEW_COLS, D_MODEL),
                   per_batch(N_HEADS, NEW_COLS)]
                  + [page_spec(D_MODEL, i) for i in range(group)]
                  + [page_spec(D_MODEL, i) for i in range(group)]
                  + [page_spec(N_HEADS, i) for i in range(group)]),
        out_specs=per_batch(dec_seq, D_MODEL),
        scratch_shapes=[pltpu.VMEM((n_rows, D_MODEL), BF16),
                        pltpu.VMEM((n_rows, D_MODEL), F32),
                        pltpu.VMEM((n_rows, LANES), F32),
                        pltpu.VMEM((n_rows, LANES), F32),
                        pltpu.VMEM((n_rows, LANES), F32)],
    )
    return pl.pallas_call(
        functools.partial(_sample_attn_kernel, group=group, dec_seq=dec_seq),
        out_shape=jax.ShapeDtypeStruct((db, dec_seq, D_MODEL), F32),
        grid_spec=grid_spec,
        compiler_params=_params("parallel", "arbitrary"),
        name="sample_attention",
    )(page_ids, q, k_new, v_new, lft_new, *([cache_kt] * group), *([cache_vt] * group), *([cache_lft] * group))


PROMPT_TM = 512
CONV_TT = 128
FFN_CHUNK = 1408
ATTN_TQ = 512
ATTN_RB = 128
ATTN_HPI = 4
PAGE_GROUP = 8


def kernel(x_prompt, x_sample, state_conv, cache_k, cache_v, cache_logf, page_table, norm_mix_g, norm_ffn_g,
           norm_final_g, conv_w_in, conv_b_in, conv_w_dw, conv_b_dw, conv_ln_g, conv_ln_b, conv_w_out, conv_b_out,
           attn_w_in, attn_b_f, attn_w_out, ffn_w_gu, ffn_w_down):
    bsz, seq, d = x_prompt.shape
    dbsz, dseq, _ = x_sample.shape
    n_sample = dbsz * dseq
    row = lambda a: a.reshape(1, -1)
    to_time_major = lambda a: jnp.transpose(a.reshape(dbsz, dseq, -1), (1, 0, 2)).reshape(n_sample, -1)
    to_batch_major = lambda a: jnp.transpose(a.reshape(dseq, dbsz, -1), (1, 0, 2))

    w_in = conv_w_in[0].astype(BF16)
    w_out = conv_w_out[0].astype(BF16)
    wgu0 = ffn_w_gu[0].astype(BF16)
    wd0 = ffn_w_down[0].astype(BF16)
    conv_args = (conv_w_dw[0], row(conv_b_dw[0]), row(conv_ln_g[0]), row(conv_ln_b[0]))

    hp = x_prompt.reshape(bsz * seq, d)
    hs = to_time_major(x_sample)

    up = conv_in(hp, row(norm_mix_g[0]), w_in, row(conv_b_in[0]), tm=PROMPT_TM)
    us = conv_in(hs, row(norm_mix_g[0]), w_in, row(conv_b_in[0]), tm=n_sample)
    zp, conv_state_p = conv_mid(up.reshape(bsz, seq, d), *conv_args, tt=CONV_TT)
    zs, conv_state_s = conv_mid_sample(us.reshape(dseq, dbsz, d), jnp.transpose(state_conv[0], (1, 0, 2)),
                                       *conv_args)
    hp = mix_ffn(hp, zp.reshape(bsz * seq, d), w_out, row(conv_b_out[0]), row(norm_ffn_g[0]), wgu0, wd0, None,
                 tm=PROMPT_TM, fc=FFN_CHUNK)
    hs = mix_ffn(hs, zs.reshape(n_sample, d), w_out, row(conv_b_out[0]), row(norm_ffn_g[0]), wgu0, wd0, None,
                 tm=n_sample, fc=FFN_CHUNK)

    w_in_t = jnp.transpose(attn_w_in[0])
    wq = attn_w_in[0, :, :d].astype(BF16)
    wkvt = w_in_t[d:3 * d].astype(BF16)
    wft = w_in_t[3 * d:].astype(BF16)
    bf_col = attn_b_f[0].reshape(N_HEADS, 1)
    wo1 = attn_w_out[0].astype(BF16)
    wgu1 = ffn_w_gu[1].astype(BF16)
    wd1 = ffn_w_down[1].astype(BF16)
    proj_args = (row(norm_mix_g[1]), wq, wkvt, wft, bf_col)

    q_slabs, ktp, vtp, lftp, k_slabs, v_slabs = attn_proj(hp.reshape(bsz, seq, d), *proj_args, tm=PROMPT_TM,
                                                          prompt=True)
    k_slabs = cumsum_into_slabs(lftp, k_slabs)
    op = flash_prompt(q_slabs, k_slabs, v_slabs, tq=ATTN_TQ, rb=ATTN_RB, hpi=ATTN_HPI)
    hp = mix_ffn(hp, op.reshape(bsz * seq, d), wo1, None, row(norm_ffn_g[1]), wgu1, wd1, row(norm_final_g),
                 tm=PROMPT_TM, fc=FFN_CHUNK)

    qs, ks, vs, lfts = attn_proj(hs.reshape(1, n_sample, d), *proj_args, tm=n_sample, prompt=False)
    ks_b = to_batch_major(ks[0])
    vs_b = to_batch_major(vs[0])
    lfts_b = jnp.transpose(lfts.reshape(N_HEADS, dseq, dbsz), (2, 0, 1))
    pad_rows = lambda a: jnp.pad(a, ((0, 0), (0, NEW_COLS - dseq), (0, 0)))
    cache_kt = jnp.transpose(cache_k, (0, 1, 3, 4, 2)).reshape(-1, d, PAGE)
    cache_vt = jnp.transpose(cache_v, (0, 1, 3, 4, 2)).reshape(-1, d, PAGE)
    cache_lft = jnp.transpose(cache_logf, (0, 1, 3, 2)).reshape(-1, N_HEADS, PAGE)
    os_ = sample_attention(page_table, to_batch_major(qs[0]), pad_rows(ks_b), pad_rows(vs_b),
                           jnp.pad(lfts_b, ((0, 0), (0, 0), (0, NEW_COLS - dseq))),
                           cache_kt, cache_vt, cache_lft, group=PAGE_GROUP)
    hs = mix_ffn(hs, to_time_major(os_), wo1, None, row(norm_ffn_g[1]), wgu1, wd1, row(norm_final_g),
                 tm=n_sample, fc=FFN_CHUNK)

    heads_t = lambda a: jnp.transpose(a.reshape(bsz, N_HEADS, HEAD_DIM, seq), (0, 3, 1, 2))[None]
    heads = lambda a: a.reshape(1, dbsz, dseq, N_HEADS, HEAD_DIM)
    return (hp.reshape(bsz, seq, d), to_batch_major(hs),
            conv_state_p[None], jnp.transpose(conv_state_s, (1, 0, 2))[None],
            heads_t(ktp), heads_t(vtp), jnp.transpose(lftp, (0, 2, 1))[None],
            heads(ks_b), heads(vs_b), jnp.transpose(lfts_b, (0, 2, 1))[None])
```

```python
import functools

import jax
import jax.numpy as jnp
from jax import lax
from jax.experimental import pallas as pl
from jax.experimental.pallas import tpu as pltpu

D_MODEL = 1024
N_HEADS = 16
HEAD_DIM = 64
N_PAIRS = N_HEADS // 2
LANES = 128
D_FF = 2816
CONV_WIDTH = 31
CONV_HIST = CONV_WIDTH - 1
HIST_PAD = 32
EPS = 1e-6
PAGE = 128
NEG = -1e30
VMEM_LIMIT_BYTES = 56 * 1024 * 1024

F32 = jnp.float32
BF16 = jnp.bfloat16
NT_DIMS = (((1,), (1,)), ((), ()))


def _params(*sem):
    return pltpu.CompilerParams(dimension_semantics=sem, vmem_limit_bytes=VMEM_LIMIT_BYTES)


def _resident(shape, layer=None):
    nd = len(shape)
    if layer is None:
        return pl.BlockSpec(shape, lambda *_: (0,) * nd, pipeline_mode=pl.Buffered(1))
    return pl.BlockSpec((None,) + shape, lambda *_: (layer,) + (0,) * nd, pipeline_mode=pl.Buffered(1))


def _rmsnorm(x, g):
    return x * lax.rsqrt(jnp.mean(x * x, axis=-1, keepdims=True) + EPS) * g


def _sigmoid(x):
    return 1.0 / (1.0 + jnp.exp(-x))


def _log_sigmoid(x):
    return jnp.minimum(x, 0.0) - jnp.log1p(jnp.exp(-jnp.abs(x)))


def _layernorm_silu(y, g, b):
    mu = jnp.mean(y, axis=-1, keepdims=True)
    yc = y - mu
    var = jnp.mean(yc * yc, axis=-1, keepdims=True)
    yn = yc * lax.rsqrt(var + EPS) * g + b
    return yn * _sigmoid(yn)


def _dot(a, b):
    return jnp.dot(a, b, preferred_element_type=F32)


def _dot_nt(a, b):
    return lax.dot_general(a, b, NT_DIMS, preferred_element_type=F32)


def _dot_exact(a, b):
    return jnp.dot(a, b, preferred_element_type=F32, precision=lax.Precision.HIGHEST)


def _upper_tri(n):
    return (lax.broadcasted_iota(jnp.int32, (n, n), 0) <= lax.broadcasted_iota(jnp.int32, (n, n), 1)).astype(F32)


def _conv_in_kernel(h_ref, g_ref, w_ref, b_ref, u_ref):
    xn = _rmsnorm(h_ref[...], g_ref[...]).astype(BF16)
    hh = _dot(xn, w_ref[...]) + b_ref[...]
    u_ref[...] = hh[:, :D_MODEL] * _sigmoid(hh[:, D_MODEL:])


def conv_in(h, g, w_bf, b, *, tm):
    m = h.shape[0]
    return pl.pallas_call(
        _conv_in_kernel,
        out_shape=jax.ShapeDtypeStruct((m, D_MODEL), F32),
        grid=(m // tm,),
        in_specs=[pl.BlockSpec((tm, D_MODEL), lambda i: (i, 0)),
                  _resident((1, D_MODEL)),
                  _resident((D_MODEL, 2 * D_MODEL)),
                  _resident((1, 2 * D_MODEL))],
        out_specs=pl.BlockSpec((tm, D_MODEL), lambda i: (i, 0)),
        compiler_params=_params("parallel"),
        name="conv_in",
    )(h, g, w_bf, b)


SUBLANES = 8
CONV_RC = 16


def _conv_mid_kernel(u_ref, wdw_ref, bdw_ref, lng_ref, lnb_ref, z_ref, st_ref, ubuf, ybuf, *, tt):
    t = pl.program_id(1)
    n_shift = ubuf.shape[1]

    @pl.when(t == 0)
    def _():
        ubuf[0, 0:HIST_PAD, :] = jnp.zeros((HIST_PAD, D_MODEL), F32)

    @pl.when(t > 0)
    def _():
        ubuf[0, 0:HIST_PAD, :] = ubuf[0, tt:tt + HIST_PAD, :]

    ubuf[0, HIST_PAD:HIST_PAD + tt, :] = u_ref[0]
    st_ref[0] = ubuf[0, tt + HIST_PAD - CONV_HIST:tt + HIST_PAD, :]
    for s in range(1, SUBLANES):
        ubuf[s, 0:n_shift - SUBLANES, :] = ubuf[0, s:s + n_shift - SUBLANES, :]

    off = HIST_PAD - CONV_HIST
    rc = min(tt, CONV_RC)

    def row_chunk(i, carry):
        r0 = pl.multiple_of(i * rc, rc)
        for l0 in range(0, D_MODEL, LANES):
            acc = jnp.broadcast_to(bdw_ref[:, l0:l0 + LANES], (rc, LANES))
            for j in range(CONV_WIDTH):
                shift, base = (j + off) % SUBLANES, (j + off) // SUBLANES * SUBLANES
                acc = acc + wdw_ref[j:j + 1, l0:l0 + LANES] * ubuf[shift, pl.ds(r0 + base, rc), l0:l0 + LANES]
            ybuf[pl.ds(r0, rc), l0:l0 + LANES] = acc
        return carry

    lax.fori_loop(0, tt // rc, row_chunk, 0)

    z_ref[0] = _layernorm_silu(ybuf[...], lng_ref[...], lnb_ref[...]).astype(z_ref.dtype)


def conv_mid(u, w_dw, b_dw, ln_g, ln_b, *, tt):
    b, t, _ = u.shape
    return pl.pallas_call(
        functools.partial(_conv_mid_kernel, tt=tt),
        out_shape=(jax.ShapeDtypeStruct((b, t, D_MODEL), BF16),
                   jax.ShapeDtypeStruct((b, CONV_HIST, D_MODEL), F32)),
        grid=(b, t // tt),
        in_specs=[pl.BlockSpec((1, tt, D_MODEL), lambda i, j: (i, j, 0)),
                  _resident((CONV_WIDTH, D_MODEL)),
                  _resident((1, D_MODEL)),
                  _resident((1, D_MODEL)),
                  _resident((1, D_MODEL))],
        out_specs=(pl.BlockSpec((1, tt, D_MODEL), lambda i, j: (i, j, 0)),
                   pl.BlockSpec((1, CONV_HIST, D_MODEL), lambda i, j: (i, 0, 0))),
        scratch_shapes=[pltpu.VMEM((SUBLANES, HIST_PAD + tt, D_MODEL), F32),
                        pltpu.VMEM((tt, D_MODEL), F32)],
        compiler_params=_params("parallel", "arbitrary"),
        name="conv_mid",
    )(u, w_dw, b_dw, ln_g, ln_b)


def _conv_mid_sample_kernel(u_ref, hist_ref, wdw_ref, bdw_ref, lng_ref, lnb_ref, z_ref, st_ref, upad):
    dseq = u_ref.shape[0]
    upad[0:CONV_HIST] = hist_ref[...]
    upad[CONV_HIST:CONV_HIST + dseq] = u_ref[...]
    st_ref[...] = upad[dseq:dseq + CONV_HIST]
    for t in range(dseq):
        acc = jnp.broadcast_to(bdw_ref[...], upad.shape[1:])
        for j in range(CONV_WIDTH):
            acc = acc + wdw_ref[j:j + 1, :] * upad[t + j]
        z_ref[t] = _layernorm_silu(acc, lng_ref[...], lnb_ref[...])


def conv_mid_sample(u, hist, w_dw, b_dw, ln_g, ln_b):
    dseq, db, _ = u.shape
    whole = lambda shape: pl.BlockSpec(shape, lambda i: (0,) * len(shape))
    return pl.pallas_call(
        _conv_mid_sample_kernel,
        out_shape=(jax.ShapeDtypeStruct((dseq, db, D_MODEL), F32),
                   jax.ShapeDtypeStruct((CONV_HIST, db, D_MODEL), F32)),
        grid=(1,),
        in_specs=[whole(u.shape), whole(hist.shape), whole(w_dw.shape), whole(b_dw.shape), whole(ln_g.shape),
                  whole(ln_b.shape)],
        out_specs=(whole((dseq, db, D_MODEL)), whole((CONV_HIST, db, D_MODEL))),
        scratch_shapes=[pltpu.VMEM((CONV_HIST + dseq, db, D_MODEL), F32)],
        compiler_params=_params("arbitrary"),
        name="conv_mid_sample",
    )(u, hist, w_dw, b_dw, ln_g, ln_b)


def _mix_ffn_kernel(*refs, fc, has_bias, final):
    h_ref, z_ref, wo_ref = refs[:3]
    refs = refs[3:]
    if has_bias:
        bo_ref, refs = refs[0], refs[1:]
    g_ref, wgu_ref, wd_ref = refs[:3]
    refs = refs[3:]
    if final:
        gf_ref, refs = refs[0], refs[1:]
    out_ref = refs[0]

    h1 = h_ref[...] + _dot(z_ref[...].astype(BF16), wo_ref[...])
    if has_bias:
        h1 = h1 + bo_ref[...]
    xn = _rmsnorm(h1, g_ref[...]).astype(BF16)
    acc = h1
    for c in range(D_FF // fc):
        gate = _dot(xn, wgu_ref[:, c * fc:(c + 1) * fc])
        up = _dot(xn, wgu_ref[:, D_FF + c * fc:D_FF + (c + 1) * fc])
        a = (gate * _sigmoid(gate) * up).astype(BF16)
        acc = acc + _dot(a, wd_ref[c * fc:(c + 1) * fc, :])
    out_ref[...] = _rmsnorm(acc, gf_ref[...]) if final else acc


def mix_ffn(h, z, wo_bf, bo, g_ffn, wgu_bf, wd_bf, g_final, *, layer, tm, fc):
    m = h.shape[0]
    has_bias = bo is not None
    final = g_final is not None
    row = pl.BlockSpec((tm, D_MODEL), lambda i: (i, 0))
    args = [h, z, wo_bf]
    specs = [row, row, _resident((D_MODEL, D_MODEL))]
    if has_bias:
        args.append(bo)
        specs.append(_resident((1, D_MODEL)))
    args += [g_ffn, wgu_bf, wd_bf]
    specs += [_resident((1, D_MODEL)), _resident((D_MODEL, 2 * D_FF), layer), _resident((D_FF, D_MODEL), layer)]
    if final:
        args.append(g_final)
        specs.append(_resident((1, D_MODEL)))
    return pl.pallas_call(
        functools.partial(_mix_ffn_kernel, fc=fc, has_bias=has_bias, final=final),
        out_shape=jax.ShapeDtypeStruct((m, D_MODEL), F32),
        grid=(m // tm,),
        in_specs=specs,
        out_specs=row,
        compiler_params=_params("parallel"),
        name="mix_ffn",
    )(*args)


C_ROW = HEAD_DIM
C_TERMS = 3
C_ROWS = 16
L_LANE = HEAD_DIM
LOG2E = 1.4426950408889634


def _attn_proj_kernel(h_ref, g_ref, wq_ref, wkvt_ref, wft_ref, bf_ref, q_ref, k_ref, v_ref, lf_ref, *slab_refs,
                      prompt):
    xn = _rmsnorm(h_ref[0], g_ref[...]).astype(BF16)
    q = _dot(xn, wq_ref[...])
    kvt = _dot_nt(wkvt_ref[...], xn)
    lf_ref[0] = _log_sigmoid(_dot_nt(wft_ref[...], xn) + bf_ref[...])
    if prompt:
        tm = q.shape[0]
        k_ref[0] = kvt[:D_MODEL]
        v_ref[0] = kvt[D_MODEL:]
        ka_ref, va_ref = slab_refs
        lane = lax.broadcasted_iota(jnp.int32, (1, LANES), 1)
        q_tail = jnp.where((lane >= C_ROW) & (lane < C_ROW + C_TERMS), -1.0, 0.0)
        row = lax.broadcasted_iota(jnp.int32, (HEAD_DIM, 1), 0)
        v_tail = jnp.broadcast_to(jnp.where(row == 0, 1.0, 0.0), (HEAD_DIM, tm)).astype(BF16)
        k_tail = jnp.zeros((HEAD_DIM, tm), BF16)
        q = q * (HEAD_DIM ** -0.5 * LOG2E)
        for p in range(N_PAIRS):
            blk = q[:, p * LANES:(p + 1) * LANES]
            q_ref[0, 2 * p] = jnp.where(lane < HEAD_DIM, blk, q_tail).astype(BF16)
            q_ref[0, 2 * p + 1] = jnp.where(lane < HEAD_DIM, pltpu.roll(blk, HEAD_DIM, 1), q_tail).astype(BF16)
        for h in range(N_HEADS):
            ka_ref[0, h, :HEAD_DIM] = kvt[h * HEAD_DIM:(h + 1) * HEAD_DIM].astype(BF16)
            ka_ref[0, h, HEAD_DIM:] = k_tail
            va_ref[0, h, :HEAD_DIM] = kvt[D_MODEL + h * HEAD_DIM:D_MODEL + (h + 1) * HEAD_DIM].astype(BF16)
            va_ref[0, h, HEAD_DIM:] = v_tail
    else:
        q_ref[0] = (q * (HEAD_DIM ** -0.5)).astype(BF16)
        kv = kvt.T
        k_ref[0] = kv[:, :D_MODEL]
        v_ref[0] = kv[:, D_MODEL:]


def attn_proj(h, g, wq_bf, wkvt_bf, wft_bf, bf_col, *, tm, prompt):
    b, s, _ = h.shape
    row = pl.BlockSpec((1, tm, D_MODEL), lambda i, j: (i, j, 0))
    col = pl.BlockSpec((1, D_MODEL, tm), lambda i, j: (i, 0, j))
    lf_spec = pl.BlockSpec((1, N_HEADS, tm), lambda i, j: (i, 0, j))
    lf_shape = jax.ShapeDtypeStruct((b, N_HEADS, s), F32)
    if prompt:
        slab = pl.BlockSpec((1, N_HEADS, LANES, tm), lambda i, j: (i, 0, 0, j))
        out_shape = [jax.ShapeDtypeStruct((b, N_HEADS, s, LANES), BF16),
                     jax.ShapeDtypeStruct((b, D_MODEL, s), F32), jax.ShapeDtypeStruct((b, D_MODEL, s), F32),
                     lf_shape,
                     jax.ShapeDtypeStruct((b, N_HEADS, LANES, s), BF16),
                     jax.ShapeDtypeStruct((b, N_HEADS, LANES, s), BF16)]
        out_specs = [pl.BlockSpec((1, N_HEADS, tm, LANES), lambda i, j: (i, 0, j, 0)), col, col, lf_spec, slab, slab]
    else:
        out_shape = [jax.ShapeDtypeStruct((b, s, D_MODEL), BF16),
                     jax.ShapeDtypeStruct((b, s, D_MODEL), F32), jax.ShapeDtypeStruct((b, s, D_MODEL), F32),
                     lf_shape]
        out_specs = [row, row, row, lf_spec]
    return pl.pallas_call(
        functools.partial(_attn_proj_kernel, prompt=prompt),
        out_shape=tuple(out_shape),
        grid=(b, s // tm),
        in_specs=[row,
                  _resident((1, D_MODEL)),
                  _resident((D_MODEL, D_MODEL)),
                  _resident((2 * D_MODEL, D_MODEL)),
                  _resident((N_HEADS, D_MODEL)),
                  _resident((N_HEADS, 1))],
        out_specs=tuple(out_specs),
        compiler_params=_params("parallel", "parallel"),
        name="attn_proj",
    )(h, g, wq_bf, wkvt_bf, wft_bf, bf_col)


CUMSUM_CHUNK = 256


def _cumsum_kernel(x_ref, slab_hbm_ref, c_ref):
    del slab_hbm_ref
    s = x_ref.shape[2]
    upper = _upper_tri(CUMSUM_CHUNK)
    row = lax.broadcasted_iota(jnp.int32, (C_ROWS, CUMSUM_CHUNK), 0)
    carry = jnp.zeros((N_HEADS, 1), F32)
    for j in range(s // CUMSUM_CHUNK):
        sl = slice(j * CUMSUM_CHUNK, (j + 1) * CUMSUM_CHUNK)
        local = _dot_exact(x_ref[0, :, sl], upper)
        c = (local + carry) * LOG2E
        carry = carry + local[:, CUMSUM_CHUNK - 1:CUMSUM_CHUNK]
        hi = c.astype(BF16).astype(F32)
        mid = (c - hi).astype(BF16).astype(F32)
        lo = c - hi - mid
        for h in range(N_HEADS):
            rows = jnp.where(row == 0, hi[h:h + 1], jnp.where(row == 1, mid[h:h + 1],
                                                               jnp.where(row == 2, lo[h:h + 1], 0.0)))
            c_ref[0, h, :, sl] = rows.astype(BF16)


def cumsum_into_slabs(lft, k_slabs):
    b, h, s = lft.shape
    return pl.pallas_call(
        _cumsum_kernel,
        out_shape=jax.ShapeDtypeStruct(k_slabs.shape, k_slabs.dtype),
        grid=(b,),
        in_specs=[pl.BlockSpec((1, h, s), lambda i: (i, 0, 0)),
                  pl.BlockSpec(memory_space=pl.ANY)],
        out_specs=pl.BlockSpec((1, h, C_ROWS, s), lambda i: (i, 0, C_ROW // C_ROWS, 0)),
        input_output_aliases={1: 0},
        compiler_params=_params("parallel"),
        name="cumsum_into_slabs",
    )(lft, k_slabs)


def _flash_kernel(qi_ref, ki_ref, q_ref, k_ref, v_ref, o_ref, acc_ref, m_ref, *, tq, rb, hpi):
    t = pl.program_id(1)
    qi = qi_ref[t]
    ki = ki_ref[t]

    @pl.when(ki == 0)
    def _():
        m_ref[...] = jnp.full(m_ref.shape, NEG, F32)
        acc_ref[...] = jnp.zeros(acc_ref.shape, F32)

    def sweep(diagonal):
        def heads(j, carry):
            chains = [(j * hpi + hh, r) for hh in range(hpi) for r in range(tq // rb)]
            rows = [slice(r * rb, (r + 1) * rb) for _, r in chains]
            nk = [(r + 1) * rb if diagonal else tq for _, r in chains]
            n = range(len(chains))
            m_prev = [m_ref[chains[i][0], rows[i], :] for i in n]
            acc = [acc_ref[chains[i][0], rows[i], :] for i in n]
            s = [_dot(q_ref[0, chains[i][0], rows[i], :], k_ref[0, chains[i][0], :, :nk[i]]) for i in n]
            m_new, pv = [], []
            for i in n:
                si = s[i]
                if diagonal:
                    qpos = lax.broadcasted_iota(jnp.int32, si.shape, 0) + chains[i][1] * rb
                    kpos = lax.broadcasted_iota(jnp.int32, si.shape, 1)
                    si = jnp.where(kpos <= qpos, si, NEG)
                m_new.append(jnp.maximum(m_prev[i], jnp.max(si, axis=1, keepdims=True)))
                pr = jnp.exp2(si - jnp.concatenate([m_new[i]] * (nk[i] // LANES), axis=1))
                pv.append(_dot_nt(pr.astype(BF16), v_ref[0, chains[i][0], :, :nk[i]]))
            for i in n:
                acc_ref[chains[i][0], rows[i], :] = jnp.exp2(m_prev[i] - m_new[i]) * acc[i] + pv[i]
                m_ref[chains[i][0], rows[i], :] = m_new[i]
            return carry

        lax.fori_loop(0, N_HEADS // hpi, heads, 0)

    @pl.when(ki < qi)
    def _():
        sweep(False)

    @pl.when(ki == qi)
    def _():
        sweep(True)
        lane = lax.broadcasted_iota(jnp.int32, (1, LANES), 1)
        for p in range(N_PAIRS):
            even = acc_ref[2 * p]
            odd = acc_ref[2 * p + 1]
            even = even / even[:, L_LANE:L_LANE + 1]
            odd = odd / odd[:, L_LANE:L_LANE + 1]
            o_ref[0, :, p * LANES:(p + 1) * LANES] = jnp.where(
                lane < HEAD_DIM, even, pltpu.roll(odd, HEAD_DIM, 1)).astype(o_ref.dtype)


def flash_prompt(q_slabs, k_slabs, v_slabs, *, tq, rb, hpi):
    b, _, s, _ = q_slabs.shape
    nq = s // tq
    qi = jnp.asarray([i for i in range(nq) for _ in range(i + 1)], jnp.int32)
    ki = jnp.asarray([j for i in range(nq) for j in range(i + 1)], jnp.int32)
    grid_spec = pltpu.PrefetchScalarGridSpec(
        num_scalar_prefetch=2,
        grid=(b, int(qi.shape[0])),
        in_specs=[pl.BlockSpec((1, N_HEADS, tq, LANES), lambda i, t, qi, ki: (i, 0, qi[t], 0)),
                  pl.BlockSpec((1, N_HEADS, LANES, tq), lambda i, t, qi, ki: (i, 0, 0, ki[t])),
                  pl.BlockSpec((1, N_HEADS, LANES, tq), lambda i, t, qi, ki: (i, 0, 0, ki[t]))],
        out_specs=pl.BlockSpec((1, tq, D_MODEL), lambda i, t, qi, ki: (i, qi[t], 0)),
        scratch_shapes=[pltpu.VMEM((N_HEADS, tq, LANES), F32),
                        pltpu.VMEM((N_HEADS, tq, LANES), F32)],
    )
    return pl.pallas_call(
        functools.partial(_flash_kernel, tq=tq, rb=rb, hpi=hpi),
        out_shape=jax.ShapeDtypeStruct((b, s, D_MODEL), BF16),
        grid_spec=grid_spec,
        compiler_params=_params("parallel", "arbitrary"),
        name="flash_prompt",
    )(qi, ki, q_slabs, k_slabs, v_slabs)


NEW_COLS = 16


def _sample_attn_kernel(pt_ref, q_ref, kn_ref, vn_ref, lfn_ref, *refs, group, dec_seq):
    kt_refs = refs[:group]
    vt_refs = refs[group:2 * group]
    lp_refs = refs[2 * group:3 * group]
    o_ref, qbd_ref, acc_ref, m_ref, l_ref, carry_ref = refs[3 * group:]
    g = pl.program_id(1)
    n_rows = dec_seq * N_HEADS

    lane_d = lax.broadcasted_iota(jnp.int32, (N_HEADS, D_MODEL), 1)
    head_d = lax.broadcasted_iota(jnp.int32, (N_HEADS, D_MODEL), 0)
    head_mask = (lane_d // HEAD_DIM == head_d).astype(F32)

    @pl.when(g == 0)
    def _():
        qf = q_ref[0].astype(F32)
        for t in range(dec_seq):
            qbd_ref[t * N_HEADS:(t + 1) * N_HEADS, :] = (head_mask * qf[t:t + 1, :]).astype(BF16)
        m_ref[...] = jnp.full(m_ref.shape, NEG, F32)
        l_ref[...] = jnp.zeros(l_ref.shape, F32)
        acc_ref[...] = jnp.zeros(acc_ref.shape, F32)
        carry_ref[...] = jnp.zeros(carry_ref.shape, F32)

    def attend(scores, lf_rows, mask, pvs):
        n = lf_rows[0].shape[1]
        local_all = _dot_exact(jnp.concatenate(lf_rows, axis=0), _upper_tri(n))
        carry = carry_ref[...]
        m_prev = m_ref[...]
        m_new = m_prev
        s = []
        for i, sc in enumerate(scores):
            local = jnp.concatenate([local_all[i * N_HEADS:(i + 1) * N_HEADS]] * dec_seq, axis=0)
            si = sc - (local + carry[:, :n])
            carry = carry + local[:, n - 1:n]
            if mask is not None:
                si = jnp.where(mask, si, NEG)
            m_new = jnp.maximum(m_new, jnp.max(si, axis=1, keepdims=True))
            s.append(si)
        alpha = jnp.exp(m_prev - m_new)
        l_new = alpha * l_ref[...]
        acc = alpha[:, :1] * acc_ref[...]
        for si, pv in zip(s, pvs):
            pr = jnp.exp(si - m_new[:, :n])
            l_new = l_new + jnp.sum(pr, axis=1, keepdims=True)
            acc = acc + pv(pr.astype(BF16))
        carry_ref[...] = carry
        m_ref[...] = m_new
        l_ref[...] = l_new
        acc_ref[...] = acc

    qbd = qbd_ref[...]
    attend([_dot(qbd, kt[...].astype(BF16)) for kt in kt_refs], [lp[...] for lp in lp_refs], None,
           [lambda pr, vt=vt: _dot_nt(pr, vt[...].astype(BF16)) for vt in vt_refs])

    @pl.when(g == pl.num_programs(1) - 1)
    def _():
        r = lax.broadcasted_iota(jnp.int32, (n_rows, NEW_COLS), 0)
        col = lax.broadcasted_iota(jnp.int32, (n_rows, NEW_COLS), 1)
        attend([_dot_nt(qbd_ref[...], kn_ref[0].astype(BF16))], [lfn_ref[0]], col <= r // N_HEADS,
               [lambda pr: _dot(pr, vn_ref[0].astype(BF16))])
        o = acc_ref[...] / l_ref[:, :1]
        rows = [jnp.sum(o[t * N_HEADS:(t + 1) * N_HEADS, :] * head_mask, axis=0, keepdims=True)
                for t in range(dec_seq)]
        o_ref[0] = jnp.concatenate(rows, axis=0)


def sample_attention(page_ids, q, k_new, v_new, lft_new, cache_kt, cache_vt, cache_lft, *, group):
    db, n_pages = page_ids.shape
    dec_seq = q.shape[1]
    n_rows = dec_seq * N_HEADS

    def page_spec(rows, i):
        return pl.BlockSpec((None, rows, PAGE), lambda b, g, pt: (pt[b, g * group + i], 0, 0))

    def per_batch(rows, width):
        return pl.BlockSpec((1, rows, width), lambda b, g, pt: (b, 0, 0))

    grid_spec = pltpu.PrefetchScalarGridSpec(
        num_scalar_prefetch=1,
        grid=(db, n_pages // group),
        in_specs=([per_batch(dec_seq, D_MODEL), per_batch(NEW_COLS, D_MODEL), per_batch(NEW_COLS, D_MODEL),
                   per_batch(N_HEADS, NEW_COLS)]
                  + [page_spec(D_MODEL, i) for i in range(group)]
                  + [page_spec(D_MODEL, i) for i in range(group)]
                  + [page_spec(N_HEADS, i) for i in range(group)]),
        out_specs=per_batch(dec_seq, D_MODEL),
        scratch_shapes=[pltpu.VMEM((n_rows, D_MODEL), BF16),
                        pltpu.VMEM((n_rows, D_MODEL), F32),
                        pltpu.VMEM((n_rows, LANES), F32),
                        pltpu.VMEM((n_rows, LANES), F32),
                        pltpu.VMEM((n_rows, LANES), F32)],
    )
    return pl.pallas_call(
        functools.partial(_sample_attn_kernel, group=group, dec_seq=dec_seq),
        out_shape=jax.ShapeDtypeStruct((db, dec_seq, D_MODEL), F32),
        grid_spec=grid_spec,
        compiler_params=_params("parallel", "arbitrary"),
        name="sample_attention",
    )(page_ids, q, k_new, v_new, lft_new, *([cache_kt] * group), *([cache_vt] * group), *([cache_lft] * group))


PROMPT_TM = 512
CONV_TT = 256
FFN_CHUNK = 1408
ATTN_TQ = 512
ATTN_RB = 128
ATTN_HPI = 16
PAGE_GROUP = 16


def kernel(x_prompt, x_sample, state_conv, cache_k, cache_v, cache_logf, page_table, norm_mix_g, norm_ffn_g,
           norm_final_g, conv_w_in, conv_b_in, conv_w_dw, conv_b_dw, conv_ln_g, conv_ln_b, conv_w_out, conv_b_out,
           attn_w_in, attn_b_f, attn_w_out, ffn_w_gu, ffn_w_down):
    bsz, seq, d = x_prompt.shape
    dbsz, dseq, _ = x_sample.shape
    n_sample = dbsz * dseq
    row = lambda a: a.reshape(1, -1)
    to_time_major = lambda a: jnp.transpose(a.reshape(dbsz, dseq, -1), (1, 0, 2)).reshape(n_sample, -1)
    to_batch_major = lambda a: jnp.transpose(a.reshape(dseq, dbsz, -1), (1, 0, 2))

    w_in = conv_w_in[0].astype(BF16)
    w_out = conv_w_out[0].astype(BF16)
    wgu = ffn_w_gu.astype(BF16)
    wd = ffn_w_down.astype(BF16)
    conv_args = (conv_w_dw[0], row(conv_b_dw[0]), row(conv_ln_g[0]), row(conv_ln_b[0]))

    hp = x_prompt.reshape(bsz * seq, d)
    hs = to_time_major(x_sample)

    up = conv_in(hp, row(norm_mix_g[0]), w_in, row(conv_b_in[0]), tm=PROMPT_TM)
    us = conv_in(hs, row(norm_mix_g[0]), w_in, row(conv_b_in[0]), tm=n_sample)
    zp, conv_state_p = conv_mid(up.reshape(bsz, seq, d), *conv_args, tt=CONV_TT)
    zs, conv_state_s = conv_mid_sample(us.reshape(dseq, dbsz, d), jnp.transpose(state_conv[0], (1, 0, 2)),
                                       *conv_args)
    hp = mix_ffn(hp, zp.reshape(bsz * seq, d), w_out, row(conv_b_out[0]), row(norm_ffn_g[0]), wgu, wd, None,
                 layer=0, tm=PROMPT_TM, fc=FFN_CHUNK)
    hs = mix_ffn(hs, zs.reshape(n_sample, d), w_out, row(conv_b_out[0]), row(norm_ffn_g[0]), wgu, wd, None,
                 layer=0, tm=n_sample, fc=FFN_CHUNK)

    w_in_t = jnp.transpose(attn_w_in[0])
    wq = attn_w_in[0, :, :d].astype(BF16)
    wkvt = w_in_t[d:3 * d].astype(BF16)
    wft = w_in_t[3 * d:].astype(BF16)
    bf_col = attn_b_f[0].reshape(N_HEADS, 1)
    wo1 = attn_w_out[0].astype(BF16)
    proj_args = (row(norm_mix_g[1]), wq, wkvt, wft, bf_col)

    q_slabs, ktp, vtp, lftp, k_slabs, v_slabs = attn_proj(hp.reshape(bsz, seq, d), *proj_args, tm=PROMPT_TM,
                                                          prompt=True)
    k_slabs = cumsum_into_slabs(lftp, k_slabs)
    op = flash_prompt(q_slabs, k_slabs, v_slabs, tq=ATTN_TQ, rb=ATTN_RB, hpi=ATTN_HPI)
    hp = mix_ffn(hp, op.reshape(bsz * seq, d), wo1, None, row(norm_ffn_g[1]), wgu, wd, row(norm_final_g),
                 layer=1, tm=PROMPT_TM, fc=FFN_CHUNK)

    qs, ks, vs, lfts = attn_proj(hs.reshape(1, n_sample, d), *proj_args, tm=n_sample, prompt=False)
    ks_b = to_batch_major(ks[0])
    vs_b = to_batch_major(vs[0])
    lfts_b = jnp.transpose(lfts.reshape(N_HEADS, dseq, dbsz), (2, 0, 1))
    pad_rows = lambda a: jnp.pad(a, ((0, 0), (0, NEW_COLS - dseq), (0, 0)))
    cache_kt = jnp.transpose(cache_k, (0, 1, 3, 4, 2)).reshape(-1, d, PAGE)
    cache_vt = jnp.transpose(cache_v, (0, 1, 3, 4, 2)).reshape(-1, d, PAGE)
    cache_lft = jnp.transpose(cache_logf, (0, 1, 3, 2)).reshape(-1, N_HEADS, PAGE)
    os_ = sample_attention(page_table, to_batch_major(qs[0]), pad_rows(ks_b), pad_rows(vs_b),
                           jnp.pad(lfts_b, ((0, 0), (0, 0), (0, NEW_COLS - dseq))),
                           cache_kt, cache_vt, cache_lft, group=PAGE_GROUP)
    hs = mix_ffn(hs, to_time_major(os_), wo1, None, row(norm_ffn_g[1]), wgu, wd, row(norm_final_g),
                 layer=1, tm=n_sample, fc=FFN_CHUNK)

    heads_t = lambda a: jnp.transpose(a.reshape(bsz, N_HEADS, HEAD_DIM, seq), (0, 3, 1, 2))[None]
    heads = lambda a: a.reshape(1, dbsz, dseq, N_HEADS, HEAD_DIM)
    return (hp.reshape(bsz, seq, d), to_batch_major(hs),
            conv_state_p[None], jnp.transpose(conv_state_s, (1, 0, 2))[None],
            heads_t(ktp), heads_t(vtp), jnp.transpose(lftp, (0, 2, 1))[None],
            heads(ks_b), heads(vs_b), jnp.transpose(lfts_b, (0, 2, 1))[None])
```

```python
import functools

import jax
import jax.numpy as jnp
from jax import lax
from jax.experimental import pallas as pl
from jax.experimental.pallas import tpu as pltpu

D_MODEL = 1024
N_HEADS = 16
HEAD_DIM = 64
N_PAIRS = N_HEADS // 2
LANES = 128
D_FF = 2816
CONV_WIDTH = 31
CONV_HIST = CONV_WIDTH - 1
HIST_PAD = 32
EPS = 1e-6
PAGE = 128
NEG = -1e30
VMEM_LIMIT_BYTES = 56 * 1024 * 1024

F32 = jnp.float32
BF16 = jnp.bfloat16
NT_DIMS = (((1,), (1,)), ((), ()))


def _params(*sem):
    return pltpu.CompilerParams(dimension_semantics=sem, vmem_limit_bytes=VMEM_LIMIT_BYTES)


def _resident(shape, layer=None):
    nd = len(shape)
    if layer is None:
        return pl.BlockSpec(shape, lambda *_: (0,) * nd, pipeline_mode=pl.Buffered(1))
    return pl.BlockSpec((None,) + shape, lambda *_: (layer,) + (0,) * nd, pipeline_mode=pl.Buffered(1))


def _rmsnorm(x, g):
    return x * lax.rsqrt(jnp.mean(x * x, axis=-1, keepdims=True) + EPS) * g


def _sigmoid(x):
    return 1.0 / (1.0 + jnp.exp(-x))


def _log_sigmoid(x):
    return jnp.minimum(x, 0.0) - jnp.log1p(jnp.exp(-jnp.abs(x)))


def _layernorm_silu(y, g, b):
    mu = jnp.mean(y, axis=-1, keepdims=True)
    yc = y - mu
    var = jnp.mean(yc * yc, axis=-1, keepdims=True)
    yn = yc * lax.rsqrt(var + EPS) * g + b
    return yn * _sigmoid(yn)


def _dot(a, b):
    return jnp.dot(a, b, preferred_element_type=F32)


def _dot_nt(a, b):
    return lax.dot_general(a, b, NT_DIMS, preferred_element_type=F32)


def _dot_exact(a, b):
    return jnp.dot(a, b, preferred_element_type=F32, precision=lax.Precision.HIGHEST)


def _upper_tri(n):
    return (lax.broadcasted_iota(jnp.int32, (n, n), 0) <= lax.broadcasted_iota(jnp.int32, (n, n), 1)).astype(F32)


def _conv_in_kernel(h_ref, g_ref, w_ref, b_ref, u_ref):
    xn = _rmsnorm(h_ref[...], g_ref[...]).astype(BF16)
    hh = _dot(xn, w_ref[...]) + b_ref[...]
    u_ref[...] = hh[:, :D_MODEL] * _sigmoid(hh[:, D_MODEL:])


def conv_in(h, g, w_bf, b, *, tm):
    m = h.shape[0]
    return pl.pallas_call(
        _conv_in_kernel,
        out_shape=jax.ShapeDtypeStruct((m, D_MODEL), F32),
        grid=(m // tm,),
        in_specs=[pl.BlockSpec((tm, D_MODEL), lambda i: (i, 0)),
                  _resident((1, D_MODEL)),
                  _resident((D_MODEL, 2 * D_MODEL)),
                  _resident((1, 2 * D_MODEL))],
        out_specs=pl.BlockSpec((tm, D_MODEL), lambda i: (i, 0)),
        compiler_params=_params("parallel"),
        name="conv_in",
    )(h, g, w_bf, b)


SUBLANES = 8
CONV_RC = 16


def _conv_mid_step(t, u_ref, wdw_ref, bdw_ref, lng_ref, lnb_ref, z_ref, st_ref, ubuf, ybuf):
    tt = u_ref.shape[1]
    n_shift = ubuf.shape[1]

    @pl.when(t == 0)
    def _():
        ubuf[0, 0:HIST_PAD, :] = jnp.zeros((HIST_PAD, D_MODEL), F32)

    @pl.when(t > 0)
    def _():
        ubuf[0, 0:HIST_PAD, :] = ubuf[0, tt:tt + HIST_PAD, :]

    ubuf[0, HIST_PAD:HIST_PAD + tt, :] = u_ref[0]
    st_ref[0] = ubuf[0, tt + HIST_PAD - CONV_HIST:tt + HIST_PAD, :]
    for s in range(1, SUBLANES):
        ubuf[s, 0:n_shift - SUBLANES, :] = ubuf[0, s:s + n_shift - SUBLANES, :]

    off = HIST_PAD - CONV_HIST
    rc = min(tt, CONV_RC)

    def row_chunk(i, carry):
        r0 = pl.multiple_of(i * rc, rc)
        for l0 in range(0, D_MODEL, LANES):
            acc = jnp.broadcast_to(bdw_ref[:, l0:l0 + LANES], (rc, LANES))
            for j in range(CONV_WIDTH):
                shift, base = (j + off) % SUBLANES, (j + off) // SUBLANES * SUBLANES
                acc = acc + wdw_ref[j:j + 1, l0:l0 + LANES] * ubuf[shift, pl.ds(r0 + base, rc), l0:l0 + LANES]
            ybuf[pl.ds(r0, rc), l0:l0 + LANES] = acc
        return carry

    lax.fori_loop(0, tt // rc, row_chunk, 0)

    z_ref[0] = _layernorm_silu(ybuf[...], lng_ref[...], lnb_ref[...]).astype(z_ref.dtype)


def _conv_mid_sample_kernel(u_ref, hist_ref, wdw_ref, bdw_ref, lng_ref, lnb_ref, z_ref, st_ref, upad):
    dseq = u_ref.shape[0]
    upad[0:CONV_HIST] = hist_ref[...]
    upad[CONV_HIST:CONV_HIST + dseq] = u_ref[...]
    st_ref[...] = upad[dseq:dseq + CONV_HIST]
    for t in range(dseq):
        acc = jnp.broadcast_to(bdw_ref[...], upad.shape[1:])
        for j in range(CONV_WIDTH):
            acc = acc + wdw_ref[j:j + 1, :] * upad[t + j]
        z_ref[t] = _layernorm_silu(acc, lng_ref[...], lnb_ref[...])


def conv_mid_sample(u, hist, w_dw, b_dw, ln_g, ln_b):
    dseq, db, _ = u.shape
    whole = lambda shape: pl.BlockSpec(shape, lambda i: (0,) * len(shape))
    return pl.pallas_call(
        _conv_mid_sample_kernel,
        out_shape=(jax.ShapeDtypeStruct((dseq, db, D_MODEL), F32),
                   jax.ShapeDtypeStruct((CONV_HIST, db, D_MODEL), F32)),
        grid=(1,),
        in_specs=[whole(u.shape), whole(hist.shape), whole(w_dw.shape), whole(b_dw.shape), whole(ln_g.shape),
                  whole(ln_b.shape)],
        out_specs=(whole((dseq, db, D_MODEL)), whole((CONV_HIST, db, D_MODEL))),
        scratch_shapes=[pltpu.VMEM((CONV_HIST + dseq, db, D_MODEL), F32)],
        compiler_params=_params("arbitrary"),
        name="conv_mid_sample",
    )(u, hist, w_dw, b_dw, ln_g, ln_b)


def _mix_ffn_kernel(*refs, fc, has_bias, final):
    h_ref, z_ref, wo_ref = refs[:3]
    refs = refs[3:]
    if has_bias:
        bo_ref, refs = refs[0], refs[1:]
    g_ref, wgu_ref, wd_ref = refs[:3]
    refs = refs[3:]
    if final:
        gf_ref, refs = refs[0], refs[1:]
    out_ref = refs[0]

    h1 = h_ref[...] + _dot(z_ref[...].astype(BF16), wo_ref[...])
    if has_bias:
        h1 = h1 + bo_ref[...]
    xn = _rmsnorm(h1, g_ref[...]).astype(BF16)
    acc = h1
    for c in range(D_FF // fc):
        gate = _dot(xn, wgu_ref[:, c * fc:(c + 1) * fc])
        up = _dot(xn, wgu_ref[:, D_FF + c * fc:D_FF + (c + 1) * fc])
        a = (gate * _sigmoid(gate) * up).astype(BF16)
        acc = acc + _dot(a, wd_ref[c * fc:(c + 1) * fc, :])
    out_ref[...] = _rmsnorm(acc, gf_ref[...]) if final else acc


def mix_ffn(h, z, wo_bf, bo, g_ffn, wgu_bf, wd_bf, g_final, *, layer, tm, fc):
    m = h.shape[0]
    has_bias = bo is not None
    final = g_final is not None
    row = pl.BlockSpec((tm, D_MODEL), lambda i: (i, 0))
    args = [h, z, wo_bf]
    specs = [row, row, _resident((D_MODEL, D_MODEL))]
    if has_bias:
        args.append(bo)
        specs.append(_resident((1, D_MODEL)))
    args += [g_ffn, wgu_bf, wd_bf]
    specs += [_resident((1, D_MODEL)), _resident((D_MODEL, 2 * D_FF), layer), _resident((D_FF, D_MODEL), layer)]
    if final:
        args.append(g_final)
        specs.append(_resident((1, D_MODEL)))
    return pl.pallas_call(
        functools.partial(_mix_ffn_kernel, fc=fc, has_bias=has_bias, final=final),
        out_shape=jax.ShapeDtypeStruct((m, D_MODEL), F32),
        grid=(m // tm,),
        in_specs=specs,
        out_specs=row,
        compiler_params=_params("parallel"),
        name="mix_ffn",
    )(*args)


C_ROW = HEAD_DIM
C_TERMS = 3
C_ROWS = 16
L_LANE = HEAD_DIM
LOG2E = 1.4426950408889634


def _attn_proj_kernel(h_ref, g_ref, wq_ref, wkvt_ref, wft_ref, bf_ref, q_ref, k_ref, v_ref, lf_ref, *slab_refs,
                      prompt):
    xn = _rmsnorm(h_ref[0], g_ref[...]).astype(BF16)
    q = _dot(xn, wq_ref[...])
    kvt = _dot_nt(wkvt_ref[...], xn)
    lf_ref[0] = _log_sigmoid(_dot_nt(wft_ref[...], xn) + bf_ref[...])
    if prompt:
        tm = q.shape[0]
        k_ref[0] = kvt[:D_MODEL]
        v_ref[0] = kvt[D_MODEL:]
        ka_ref, va_ref = slab_refs
        lane = lax.broadcasted_iota(jnp.int32, (1, LANES), 1)
        q_tail = jnp.where((lane >= C_ROW) & (lane < C_ROW + C_TERMS), -1.0, 0.0)
        row = lax.broadcasted_iota(jnp.int32, (HEAD_DIM, 1), 0)
        v_tail = jnp.broadcast_to(jnp.where(row == 0, 1.0, 0.0), (HEAD_DIM, tm)).astype(BF16)
        k_tail = jnp.zeros((HEAD_DIM, tm), BF16)
        q = q * (HEAD_DIM ** -0.5 * LOG2E)
        for p in range(N_PAIRS):
            blk = q[:, p * LANES:(p + 1) * LANES]
            q_ref[0, 2 * p] = jnp.where(lane < HEAD_DIM, blk, q_tail).astype(BF16)
            q_ref[0, 2 * p + 1] = jnp.where(lane < HEAD_DIM, pltpu.roll(blk, HEAD_DIM, 1), q_tail).astype(BF16)
        for h in range(N_HEADS):
            ka_ref[0, h, :HEAD_DIM] = kvt[h * HEAD_DIM:(h + 1) * HEAD_DIM].astype(BF16)
            ka_ref[0, h, HEAD_DIM:] = k_tail
            va_ref[0, h, :HEAD_DIM] = kvt[D_MODEL + h * HEAD_DIM:D_MODEL + (h + 1) * HEAD_DIM].astype(BF16)
            va_ref[0, h, HEAD_DIM:] = v_tail
    else:
        q_ref[0] = (q * (HEAD_DIM ** -0.5)).astype(BF16)
        kv = kvt.T
        k_ref[0] = kv[:, :D_MODEL]
        v_ref[0] = kv[:, D_MODEL:]


def attn_proj(h, g, wq_bf, wkvt_bf, wft_bf, bf_col, *, tm, prompt):
    b, s, _ = h.shape
    row = pl.BlockSpec((1, tm, D_MODEL), lambda i, j: (i, j, 0))
    col = pl.BlockSpec((1, D_MODEL, tm), lambda i, j: (i, 0, j))
    lf_spec = pl.BlockSpec((1, N_HEADS, tm), lambda i, j: (i, 0, j))
    lf_shape = jax.ShapeDtypeStruct((b, N_HEADS, s), F32)
    if prompt:
        slab = pl.BlockSpec((1, N_HEADS, LANES, tm), lambda i, j: (i, 0, 0, j))
        out_shape = [jax.ShapeDtypeStruct((b, N_HEADS, s, LANES), BF16),
                     jax.ShapeDtypeStruct((b, D_MODEL, s), F32), jax.ShapeDtypeStruct((b, D_MODEL, s), F32),
                     lf_shape,
                     jax.ShapeDtypeStruct((b, N_HEADS, LANES, s), BF16),
                     jax.ShapeDtypeStruct((b, N_HEADS, LANES, s), BF16)]
        out_specs = [pl.BlockSpec((1, N_HEADS, tm, LANES), lambda i, j: (i, 0, j, 0)), col, col, lf_spec, slab, slab]
    else:
        out_shape = [jax.ShapeDtypeStruct((b, s, D_MODEL), BF16),
                     jax.ShapeDtypeStruct((b, s, D_MODEL), F32), jax.ShapeDtypeStruct((b, s, D_MODEL), F32),
                     lf_shape]
        out_specs = [row, row, row, lf_spec]
    return pl.pallas_call(
        functools.partial(_attn_proj_kernel, prompt=prompt),
        out_shape=tuple(out_shape),
        grid=(b, s // tm),
        in_specs=[row,
                  _resident((1, D_MODEL)),
                  _resident((D_MODEL, D_MODEL)),
                  _resident((2 * D_MODEL, D_MODEL)),
                  _resident((N_HEADS, D_MODEL)),
                  _resident((N_HEADS, 1))],
        out_specs=tuple(out_specs),
        compiler_params=_params("parallel", "parallel"),
        name="attn_proj",
    )(h, g, wq_bf, wkvt_bf, wft_bf, bf_col)


CUMSUM_CHUNK = 256


def _cumsum_kernel(x_ref, slab_hbm_ref, c_ref):
    del slab_hbm_ref
    s = x_ref.shape[2]
    upper = _upper_tri(CUMSUM_CHUNK)
    row = lax.broadcasted_iota(jnp.int32, (C_ROWS, CUMSUM_CHUNK), 0)
    carry = jnp.zeros((N_HEADS, 1), F32)
    for j in range(s // CUMSUM_CHUNK):
        sl = slice(j * CUMSUM_CHUNK, (j + 1) * CUMSUM_CHUNK)
        local = _dot_exact(x_ref[0, :, sl], upper)
        c = (local + carry) * LOG2E
        carry = carry + local[:, CUMSUM_CHUNK - 1:CUMSUM_CHUNK]
        hi = c.astype(BF16).astype(F32)
        mid = (c - hi).astype(BF16).astype(F32)
        lo = c - hi - mid
        for h in range(N_HEADS):
            rows = jnp.where(row == 0, hi[h:h + 1], jnp.where(row == 1, mid[h:h + 1],
                                                               jnp.where(row == 2, lo[h:h + 1], 0.0)))
            c_ref[0, h, :, sl] = rows.astype(BF16)


def cumsum_into_slabs(lft, k_slabs):
    b, h, s = lft.shape
    return pl.pallas_call(
        _cumsum_kernel,
        out_shape=jax.ShapeDtypeStruct(k_slabs.shape, k_slabs.dtype),
        grid=(b,),
        in_specs=[pl.BlockSpec((1, h, s), lambda i: (i, 0, 0)),
                  pl.BlockSpec(memory_space=pl.ANY)],
        out_specs=pl.BlockSpec((1, h, C_ROWS, s), lambda i: (i, 0, C_ROW // C_ROWS, 0)),
        input_output_aliases={1: 0},
        compiler_params=_params("parallel"),
        name="cumsum_into_slabs",
    )(lft, k_slabs)


def _flash_kernel(qi_ref, ki_ref, q_ref, k_ref, v_ref, o_ref, acc_ref, m_ref, *, tq, rb, hpi):
    t = pl.program_id(1)
    qi = qi_ref[t]
    ki = ki_ref[t]

    @pl.when(ki == 0)
    def _():
        m_ref[...] = jnp.full(m_ref.shape, NEG, F32)
        acc_ref[...] = jnp.zeros(acc_ref.shape, F32)

    def sweep(diagonal):
        def heads(j, carry):
            chains = [(j * hpi + hh, r) for hh in range(hpi) for r in range(tq // rb)]
            rows = [slice(r * rb, (r + 1) * rb) for _, r in chains]
            nk = [(r + 1) * rb if diagonal else tq for _, r in chains]
            n = range(len(chains))
            m_prev = [m_ref[chains[i][0], rows[i], :] for i in n]
            acc = [acc_ref[chains[i][0], rows[i], :] for i in n]
            s = [_dot(q_ref[0, chains[i][0], rows[i], :], k_ref[0, chains[i][0], :, :nk[i]]) for i in n]
            m_new, pv = [], []
            for i in n:
                si = s[i]
                if diagonal:
                    qpos = lax.broadcasted_iota(jnp.int32, si.shape, 0) + chains[i][1] * rb
                    kpos = lax.broadcasted_iota(jnp.int32, si.shape, 1)
                    si = jnp.where(kpos <= qpos, si, NEG)
                m_new.append(jnp.maximum(m_prev[i], jnp.max(si, axis=1, keepdims=True)))
                pr = jnp.exp2(si - jnp.concatenate([m_new[i]] * (nk[i] // LANES), axis=1))
                pv.append(_dot_nt(pr.astype(BF16), v_ref[0, chains[i][0], :, :nk[i]]))
            for i in n:
                acc_ref[chains[i][0], rows[i], :] = jnp.exp2(m_prev[i] - m_new[i]) * acc[i] + pv[i]
                m_ref[chains[i][0], rows[i], :] = m_new[i]
            return carry

        lax.fori_loop(0, N_HEADS // hpi, heads, 0)

    @pl.when(ki < qi)
    def _():
        sweep(False)

    @pl.when(ki == qi)
    def _():
        sweep(True)
        lane = lax.broadcasted_iota(jnp.int32, (1, LANES), 1)
        for p in range(N_PAIRS):
            even = acc_ref[2 * p]
            odd = acc_ref[2 * p + 1]
            even = even / even[:, L_LANE:L_LANE + 1]
            odd = odd / odd[:, L_LANE:L_LANE + 1]
            o_ref[0, :, p * LANES:(p + 1) * LANES] = jnp.where(
                lane < HEAD_DIM, even, pltpu.roll(odd, HEAD_DIM, 1)).astype(o_ref.dtype)


def flash_prompt(q_slabs, k_slabs, v_slabs, *, tq, rb, hpi):
    b, _, s, _ = q_slabs.shape
    nq = s // tq
    qi = jnp.asarray([i for i in range(nq) for _ in range(i + 1)], jnp.int32)
    ki = jnp.asarray([j for i in range(nq) for j in range(i + 1)], jnp.int32)
    grid_spec = pltpu.PrefetchScalarGridSpec(
        num_scalar_prefetch=2,
        grid=(b, int(qi.shape[0])),
        in_specs=[pl.BlockSpec((1, N_HEADS, tq, LANES), lambda i, t, qi, ki: (i, 0, qi[t], 0)),
                  pl.BlockSpec((1, N_HEADS, LANES, tq), lambda i, t, qi, ki: (i, 0, 0, ki[t])),
                  pl.BlockSpec((1, N_HEADS, LANES, tq), lambda i, t, qi, ki: (i, 0, 0, ki[t]))],
        out_specs=pl.BlockSpec((1, tq, D_MODEL), lambda i, t, qi, ki: (i, qi[t], 0)),
        scratch_shapes=[pltpu.VMEM((N_HEADS, tq, LANES), F32),
                        pltpu.VMEM((N_HEADS, tq, LANES), F32)],
    )
    return pl.pallas_call(
        functools.partial(_flash_kernel, tq=tq, rb=rb, hpi=hpi),
        out_shape=jax.ShapeDtypeStruct((b, s, D_MODEL), BF16),
        grid_spec=grid_spec,
        compiler_params=_params("parallel", "arbitrary"),
        name="flash_prompt",
    )(qi, ki, q_slabs, k_slabs, v_slabs)


NEW_COLS = 16


def _sample_attn_step(g, last, q_ref, kn_ref, vn_ref, lfn_ref, kt_refs, vt_refs, lp_refs,
                      o_ref, qbd_ref, acc_ref, m_ref, l_ref, carry_ref):
    dec_seq = q_ref.shape[1]
    n_rows = dec_seq * N_HEADS

    lane_d = lax.broadcasted_iota(jnp.int32, (N_HEADS, D_MODEL), 1)
    head_d = lax.broadcasted_iota(jnp.int32, (N_HEADS, D_MODEL), 0)
    head_mask = (lane_d // HEAD_DIM == head_d).astype(F32)

    @pl.when(g == 0)
    def _():
        qf = q_ref[0].astype(F32)
        for t in range(dec_seq):
            qbd_ref[t * N_HEADS:(t + 1) * N_HEADS, :] = (head_mask * qf[t:t + 1, :]).astype(BF16)
        m_ref[...] = jnp.full(m_ref.shape, NEG, F32)
        l_ref[...] = jnp.zeros(l_ref.shape, F32)
        acc_ref[...] = jnp.zeros(acc_ref.shape, F32)
        carry_ref[...] = jnp.zeros(carry_ref.shape, F32)

    def attend(scores, lf_rows, mask, pvs):
        n = lf_rows[0].shape[1]
        local_all = _dot_exact(jnp.concatenate(lf_rows, axis=0), _upper_tri(n))
        carry = carry_ref[...]
        m_prev = m_ref[...]
        m_new = m_prev
        s = []
        for i, sc in enumerate(scores):
            local = jnp.concatenate([local_all[i * N_HEADS:(i + 1) * N_HEADS]] * dec_seq, axis=0)
            si = sc - (local + carry[:, :n])
            carry = carry + local[:, n - 1:n]
            if mask is not None:
                si = jnp.where(mask, si, NEG)
            m_new = jnp.maximum(m_new, jnp.max(si, axis=1, keepdims=True))
            s.append(si)
        alpha = jnp.exp(m_prev - m_new)
        l_new = alpha * l_ref[...]
        acc = alpha[:, :1] * acc_ref[...]
        for si, pv in zip(s, pvs):
            pr = jnp.exp(si - m_new[:, :n])
            l_new = l_new + jnp.sum(pr, axis=1, keepdims=True)
            acc = acc + pv(pr.astype(BF16))
        carry_ref[...] = carry
        m_ref[...] = m_new
        l_ref[...] = l_new
        acc_ref[...] = acc

    qbd = qbd_ref[...]
    attend([_dot(qbd, kt[...].astype(BF16)) for kt in kt_refs], [lp[...] for lp in lp_refs], None,
           [lambda pr, vt=vt: _dot_nt(pr, vt[...].astype(BF16)) for vt in vt_refs])

    @pl.when(last)
    def _():
        r = lax.broadcasted_iota(jnp.int32, (n_rows, NEW_COLS), 0)
        col = lax.broadcasted_iota(jnp.int32, (n_rows, NEW_COLS), 1)
        attend([_dot_nt(qbd_ref[...], kn_ref[0].astype(BF16))], [lfn_ref[0]], col <= r // N_HEADS,
               [lambda pr: _dot(pr, vn_ref[0].astype(BF16))])
        o = acc_ref[...] / l_ref[:, :1]
        rows = [jnp.sum(o[t * N_HEADS:(t + 1) * N_HEADS, :] * head_mask, axis=0, keepdims=True)
                for t in range(dec_seq)]
        o_ref[0] = jnp.concatenate(rows, axis=0)


def _sample_attn_conv_kernel(pt_ref, q_ref, kn_ref, vn_ref, lfn_ref, *refs, group, conv_tiles):
    kt_refs, vt_refs, lp_refs = refs[:group], refs[group:2 * group], refs[2 * group:3 * group]
    conv_in_refs = refs[3 * group:3 * group + 5]
    o_ref, z_ref, st_ref = refs[3 * group + 5:3 * group + 8]
    attn_scratch = refs[3 * group + 8:3 * group + 13]
    conv_scratch = refs[3 * group + 13:]
    g = pl.program_id(1)
    step = pl.program_id(0) * pl.num_programs(1) + g
    _sample_attn_step(g, g == pl.num_programs(1) - 1, q_ref, kn_ref, vn_ref, lfn_ref, kt_refs, vt_refs, lp_refs,
                      o_ref, *attn_scratch)
    _conv_mid_step(step % conv_tiles, *conv_in_refs, z_ref, st_ref, *conv_scratch)


def sample_attention_and_conv_mid(page_ids, q, k_new, v_new, lft_new, cache_kt, cache_vt, cache_lft,
                                  u, w_dw, b_dw, ln_g, ln_b, *, group):
    db, n_pages = page_ids.shape
    dec_seq = q.shape[1]
    n_rows = dec_seq * N_HEADS
    n_groups = n_pages // group
    bsz, seq, _ = u.shape
    tt = bsz * seq // (db * n_groups)
    conv_tiles = seq // tt
    assert tt * db * n_groups == bsz * seq and conv_tiles * tt == seq and tt % SUBLANES == 0 and tt >= HIST_PAD

    def page_spec(rows, i):
        return pl.BlockSpec((None, rows, PAGE), lambda b, g, pt: (pt[b, g * group + i], 0, 0))

    def per_batch(rows, width):
        return pl.BlockSpec((1, rows, width), lambda b, g, pt: (b, 0, 0))

    conv_tile = pl.BlockSpec((1, tt, D_MODEL), lambda b, g, pt: ((b * n_groups + g) // conv_tiles,
                                                                  (b * n_groups + g) % conv_tiles, 0))
    grid_spec = pltpu.PrefetchScalarGridSpec(
        num_scalar_prefetch=1,
        grid=(db, n_groups),
        in_specs=([per_batch(dec_seq, D_MODEL), per_batch(NEW_COLS, D_MODEL), per_batch(NEW_COLS, D_MODEL),
                   per_batch(N_HEADS, NEW_COLS)]
                  + [page_spec(D_MODEL, i) for i in range(group)]
                  + [page_spec(D_MODEL, i) for i in range(group)]
                  + [page_spec(N_HEADS, i) for i in range(group)]
                  + [conv_tile, _resident((CONV_WIDTH, D_MODEL)), _resident((1, D_MODEL)),
                     _resident((1, D_MODEL)), _resident((1, D_MODEL))]),
        out_specs=(per_batch(dec_seq, D_MODEL), conv_tile,
                   pl.BlockSpec((1, CONV_HIST, D_MODEL), lambda b, g, pt: ((b * n_groups + g) // conv_tiles, 0, 0))),
        scratch_shapes=[pltpu.VMEM((n_rows, D_MODEL), BF16),
                        pltpu.VMEM((n_rows, D_MODEL), F32),
                        pltpu.VMEM((n_rows, LANES), F32),
                        pltpu.VMEM((n_rows, LANES), F32),
                        pltpu.VMEM((n_rows, LANES), F32),
                        pltpu.VMEM((SUBLANES, HIST_PAD + tt, D_MODEL), F32),
                        pltpu.VMEM((tt, D_MODEL), F32)],
    )
    return pl.pallas_call(
        functools.partial(_sample_attn_conv_kernel, group=group, conv_tiles=conv_tiles),
        out_shape=(jax.ShapeDtypeStruct((db, dec_seq, D_MODEL), F32),
                   jax.ShapeDtypeStruct((bsz, seq, D_MODEL), BF16),
                   jax.ShapeDtypeStruct((bsz, CONV_HIST, D_MODEL), F32)),
        grid_spec=grid_spec,
        compiler_params=_params("arbitrary", "arbitrary"),
        name="sample_attention_and_conv_mid",
    )(page_ids, q, k_new, v_new, lft_new, *([cache_kt] * group), *([cache_vt] * group), *([cache_lft] * group),
      u, w_dw, b_dw, ln_g, ln_b)


PROMPT_TM = 512
FFN_CHUNK = 2816
ATTN_TQ = 512
ATTN_RB = 128
ATTN_HPI = 16
PAGE_GROUP = 16


def kernel(x_prompt, x_sample, state_conv, cache_k, cache_v, cache_logf, page_table, norm_mix_g, norm_ffn_g,
           norm_final_g, conv_w_in, conv_b_in, conv_w_dw, conv_b_dw, conv_ln_g, conv_ln_b, conv_w_out, conv_b_out,
           attn_w_in, attn_b_f, attn_w_out, ffn_w_gu, ffn_w_down):
    bsz, seq, d = x_prompt.shape
    dbsz, dseq, _ = x_sample.shape
    n_sample = dbsz * dseq
    row = lambda a: a.reshape(1, -1)
    to_time_major = lambda a: jnp.transpose(a.reshape(dbsz, dseq, -1), (1, 0, 2)).reshape(n_sample, -1)
    to_batch_major = lambda a: jnp.transpose(a.reshape(dseq, dbsz, -1), (1, 0, 2))

    w_in = conv_w_in[0].astype(BF16)
    w_out = conv_w_out[0].astype(BF16)
    wgu = ffn_w_gu.astype(BF16)
    wd = ffn_w_down.astype(BF16)
    conv_args = (conv_w_dw[0], row(conv_b_dw[0]), row(conv_ln_g[0]), row(conv_ln_b[0]))
    w_in_t = jnp.transpose(attn_w_in[0])
    wq = attn_w_in[0, :, :d].astype(BF16)
    wkvt = w_in_t[d:3 * d].astype(BF16)
    wft = w_in_t[3 * d:].astype(BF16)
    bf_col = attn_b_f[0].reshape(N_HEADS, 1)
    wo1 = attn_w_out[0].astype(BF16)
    proj_args = (row(norm_mix_g[1]), wq, wkvt, wft, bf_col)

    hp = x_prompt.reshape(bsz * seq, d)
    hs = to_time_major(x_sample)

    us = conv_in(hs, row(norm_mix_g[0]), w_in, row(conv_b_in[0]), tm=n_sample)
    zs, conv_state_s = conv_mid_sample(us.reshape(dseq, dbsz, d), jnp.transpose(state_conv[0], (1, 0, 2)),
                                       *conv_args)
    hs = mix_ffn(hs, zs.reshape(n_sample, d), w_out, row(conv_b_out[0]), row(norm_ffn_g[0]), wgu, wd, None,
                 layer=0, tm=n_sample, fc=FFN_CHUNK)
    qs, ks, vs, lfts = attn_proj(hs.reshape(1, n_sample, d), *proj_args, tm=n_sample, prompt=False)
    ks_b = to_batch_major(ks[0])
    vs_b = to_batch_major(vs[0])
    lfts_b = jnp.transpose(lfts.reshape(N_HEADS, dseq, dbsz), (2, 0, 1))
    pad_rows = lambda a: jnp.pad(a, ((0, 0), (0, NEW_COLS - dseq), (0, 0)))
    cache_kt = jnp.transpose(cache_k, (0, 1, 3, 4, 2)).reshape(-1, d, PAGE)
    cache_vt = jnp.transpose(cache_v, (0, 1, 3, 4, 2)).reshape(-1, d, PAGE)
    cache_lft = jnp.transpose(cache_logf, (0, 1, 3, 2)).reshape(-1, N_HEADS, PAGE)

    up = conv_in(hp, row(norm_mix_g[0]), w_in, row(conv_b_in[0]), tm=PROMPT_TM)
    os_, zp, conv_state_p = sample_attention_and_conv_mid(
        page_table, to_batch_major(qs[0]), pad_rows(ks_b), pad_rows(vs_b),
        jnp.pad(lfts_b, ((0, 0), (0, 0), (0, NEW_COLS - dseq))), cache_kt, cache_vt, cache_lft,
        up.reshape(bsz, seq, d), *conv_args, group=PAGE_GROUP)
    hp = mix_ffn(hp, zp.reshape(bsz * seq, d), w_out, row(conv_b_out[0]), row(norm_ffn_g[0]), wgu, wd, None,
                 layer=0, tm=PROMPT_TM, fc=FFN_CHUNK)

    q_slabs, ktp, vtp, lftp, k_slabs, v_slabs = attn_proj(hp.reshape(bsz, seq, d), *proj_args, tm=PROMPT_TM,
                                                          prompt=True)
    k_slabs = cumsum_into_slabs(lftp, k_slabs)
    op = flash_prompt(q_slabs, k_slabs, v_slabs, tq=ATTN_TQ, rb=ATTN_RB, hpi=ATTN_HPI)
    hp = mix_ffn(hp, op.reshape(bsz * seq, d), wo1, None, row(norm_ffn_g[1]), wgu, wd, row(norm_final_g),
                 layer=1, tm=PROMPT_TM, fc=FFN_CHUNK)
    hs = mix_ffn(hs, to_time_major(os_), wo1, None, row(norm_ffn_g[1]), wgu, wd, row(norm_final_g),
                 layer=1, tm=n_sample, fc=FFN_CHUNK)

    heads_t = lambda a: jnp.transpose(a.reshape(bsz, N_HEADS, HEAD_DIM, seq), (0, 3, 1, 2))[None]
    heads = lambda a: a.reshape(1, dbsz, dseq, N_HEADS, HEAD_DIM)
    return (hp.reshape(bsz, seq, d), to_batch_major(hs),
            conv_state_p[None], jnp.transpose(conv_state_s, (1, 0, 2))[None],
            heads_t(ktp), heads_t(vtp), jnp.transpose(lftp, (0, 2, 1))[None],
            heads(ks_b), heads(vs_b), jnp.transpose(lfts_b, (0, 2, 1))[None])
```

```python
import functools

import jax
import jax.numpy as jnp
from jax import lax
from jax.experimental import pallas as pl
from jax.experimental.pallas import tpu as pltpu

D_MODEL = 1024
N_HEADS = 16
HEAD_DIM = 64
N_PAIRS = N_HEADS // 2
LANES = 128
D_FF = 2816
CONV_WIDTH = 31
CONV_HIST = CONV_WIDTH - 1
HIST_PAD = 32
EPS = 1e-6
PAGE = 128
NEG = -1e30
VMEM_LIMIT_BYTES = 56 * 1024 * 1024

F32 = jnp.float32
BF16 = jnp.bfloat16
NT_DIMS = (((1,), (1,)), ((), ()))
TN_DIMS = (((0,), (0,)), ((), ()))


def _params(*sem):
    return pltpu.CompilerParams(dimension_semantics=sem, vmem_limit_bytes=VMEM_LIMIT_BYTES)


def _resident(shape, layer=None):
    nd = len(shape)
    if layer is None:
        return pl.BlockSpec(shape, lambda *_: (0,) * nd, pipeline_mode=pl.Buffered(1))
    return pl.BlockSpec((None,) + shape, lambda *_: (layer,) + (0,) * nd, pipeline_mode=pl.Buffered(1))


def _rmsnorm(x, g):
    return x * lax.rsqrt(jnp.mean(x * x, axis=-1, keepdims=True) + EPS) * g


def _sigmoid(x):
    return 1.0 / (1.0 + jnp.exp(-x))


def _log_sigmoid(x):
    return jnp.minimum(x, 0.0) - jnp.log1p(jnp.exp(-jnp.abs(x)))


def _layernorm_silu(y, g, b):
    mu = jnp.mean(y, axis=-1, keepdims=True)
    yc = y - mu
    var = jnp.mean(yc * yc, axis=-1, keepdims=True)
    yn = yc * lax.rsqrt(var + EPS) * g + b
    return yn * _sigmoid(yn)


def _dot(a, b):
    return jnp.dot(a, b, preferred_element_type=F32)


def _dot_nt(a, b):
    return lax.dot_general(a, b, NT_DIMS, preferred_element_type=F32)


def _dot_exact(a, b):
    return jnp.dot(a, b, preferred_element_type=F32, precision=lax.Precision.HIGHEST)


def _upper_tri(n):
    return (lax.broadcasted_iota(jnp.int32, (n, n), 0) <= lax.broadcasted_iota(jnp.int32, (n, n), 1)).astype(F32)


def _conv_in_kernel(h_ref, g_ref, w_ref, b_ref, u_ref):
    xn = _rmsnorm(h_ref[...], g_ref[...]).astype(BF16)
    hh = _dot(xn, w_ref[...]) + b_ref[...]
    u_ref[...] = hh[:, :D_MODEL] * _sigmoid(hh[:, D_MODEL:])


def conv_in(h, g, w_bf, b, *, tm):
    m = h.shape[0]
    return pl.pallas_call(
        _conv_in_kernel,
        out_shape=jax.ShapeDtypeStruct((m, D_MODEL), F32),
        grid=(m // tm,),
        in_specs=[pl.BlockSpec((tm, D_MODEL), lambda i: (i, 0)),
                  _resident((1, D_MODEL)),
                  _resident((D_MODEL, 2 * D_MODEL)),
                  _resident((1, 2 * D_MODEL))],
        out_specs=pl.BlockSpec((tm, D_MODEL), lambda i: (i, 0)),
        compiler_params=_params("parallel"),
        name="conv_in",
    )(h, g, w_bf, b)


SUBLANES = 8
CONV_RC = 16


def _conv_mid_step(t, u_ref, wdw_ref, bdw_ref, lng_ref, lnb_ref, z_ref, st_ref, ubuf, ybuf):
    tt = u_ref.shape[1]
    n_shift = ubuf.shape[1]

    @pl.when(t == 0)
    def _():
        ubuf[0, 0:HIST_PAD, :] = jnp.zeros((HIST_PAD, D_MODEL), F32)

    @pl.when(t > 0)
    def _():
        ubuf[0, 0:HIST_PAD, :] = ubuf[0, tt:tt + HIST_PAD, :]

    ubuf[0, HIST_PAD:HIST_PAD + tt, :] = u_ref[0]
    st_ref[0] = ubuf[0, tt + HIST_PAD - CONV_HIST:tt + HIST_PAD, :]
    for s in range(1, SUBLANES):
        ubuf[s, 0:n_shift - SUBLANES, :] = ubuf[0, s:s + n_shift - SUBLANES, :]

    off = HIST_PAD - CONV_HIST
    rc = min(tt, CONV_RC)

    def row_chunk(i, carry):
        r0 = pl.multiple_of(i * rc, rc)
        for l0 in range(0, D_MODEL, LANES):
            acc = jnp.broadcast_to(bdw_ref[:, l0:l0 + LANES], (rc, LANES))
            for j in range(CONV_WIDTH):
                shift, base = (j + off) % SUBLANES, (j + off) // SUBLANES * SUBLANES
                acc = acc + wdw_ref[j:j + 1, l0:l0 + LANES] * ubuf[shift, pl.ds(r0 + base, rc), l0:l0 + LANES]
            ybuf[pl.ds(r0, rc), l0:l0 + LANES] = acc
        return carry

    lax.fori_loop(0, tt // rc, row_chunk, 0)

    z_ref[0] = _layernorm_silu(ybuf[...], lng_ref[...], lnb_ref[...]).astype(z_ref.dtype)


def _conv_mid_sample_kernel(u_ref, hist_ref, wdw_ref, bdw_ref, lng_ref, lnb_ref, z_ref, st_ref, upad):
    dseq = u_ref.shape[0]
    upad[0:CONV_HIST] = hist_ref[...]
    upad[CONV_HIST:CONV_HIST + dseq] = u_ref[...]
    st_ref[...] = upad[dseq:dseq + CONV_HIST]
    for t in range(dseq):
        acc = jnp.broadcast_to(bdw_ref[...], upad.shape[1:])
        for j in range(CONV_WIDTH):
            acc = acc + wdw_ref[j:j + 1, :] * upad[t + j]
        z_ref[t] = _layernorm_silu(acc, lng_ref[...], lnb_ref[...])


def conv_mid_sample(u, hist, w_dw, b_dw, ln_g, ln_b):
    dseq, db, _ = u.shape
    whole = lambda shape: pl.BlockSpec(shape, lambda i: (0,) * len(shape))
    return pl.pallas_call(
        _conv_mid_sample_kernel,
        out_shape=(jax.ShapeDtypeStruct((dseq, db, D_MODEL), F32),
                   jax.ShapeDtypeStruct((CONV_HIST, db, D_MODEL), F32)),
        grid=(1,),
        in_specs=[whole(u.shape), whole(hist.shape), whole(w_dw.shape), whole(b_dw.shape), whole(ln_g.shape),
                  whole(ln_b.shape)],
        out_specs=(whole((dseq, db, D_MODEL)), whole((CONV_HIST, db, D_MODEL))),
        scratch_shapes=[pltpu.VMEM((CONV_HIST + dseq, db, D_MODEL), F32)],
        compiler_params=_params("arbitrary"),
        name="conv_mid_sample",
    )(u, hist, w_dw, b_dw, ln_g, ln_b)


def _mix_ffn_kernel(*refs, fc, has_bias, final):
    h_ref, z_ref, wo_ref = refs[:3]
    refs = refs[3:]
    if has_bias:
        bo_ref, refs = refs[0], refs[1:]
    g_ref, wgu_ref, wd_ref = refs[:3]
    refs = refs[3:]
    if final:
        gf_ref, refs = refs[0], refs[1:]
    out_ref = refs[0]

    h1 = h_ref[...] + _dot(z_ref[...].astype(BF16), wo_ref[...])
    if has_bias:
        h1 = h1 + bo_ref[...]
    xn = _rmsnorm(h1, g_ref[...]).astype(BF16)
    acc = h1
    for c in range(D_FF // fc):
        gate = _dot(xn, wgu_ref[:, c * fc:(c + 1) * fc])
        up = _dot(xn, wgu_ref[:, D_FF + c * fc:D_FF + (c + 1) * fc])
        a = (gate * _sigmoid(gate) * up).astype(BF16)
        acc = acc + _dot(a, wd_ref[c * fc:(c + 1) * fc, :])
    out_ref[...] = _rmsnorm(acc, gf_ref[...]) if final else acc


def mix_ffn(h, z, wo_bf, bo, g_ffn, wgu_bf, wd_bf, g_final, *, layer, tm, fc):
    m = h.shape[0]
    has_bias = bo is not None
    final = g_final is not None
    row = pl.BlockSpec((tm, D_MODEL), lambda i: (i, 0))
    args = [h, z, wo_bf]
    specs = [row, row, _resident((D_MODEL, D_MODEL))]
    if has_bias:
        args.append(bo)
        specs.append(_resident((1, D_MODEL)))
    args += [g_ffn, wgu_bf, wd_bf]
    specs += [_resident((1, D_MODEL)), _resident((D_MODEL, 2 * D_FF), layer), _resident((D_FF, D_MODEL), layer)]
    if final:
        args.append(g_final)
        specs.append(_resident((1, D_MODEL)))
    return pl.pallas_call(
        functools.partial(_mix_ffn_kernel, fc=fc, has_bias=has_bias, final=final),
        out_shape=jax.ShapeDtypeStruct((m, D_MODEL), F32),
        grid=(m // tm,),
        in_specs=specs,
        out_specs=row,
        compiler_params=_params("parallel"),
        name="mix_ffn",
    )(*args)


C_ROW = HEAD_DIM
C_TERMS = 3
C_ROWS = 16
L_ROW = HEAD_DIM
V_ROWS = HEAD_DIM + 16
LOG2E = 1.4426950408889634


def _attn_proj_prompt_kernel(h_ref, g_ref, wt_ref, wft_ref, bf_ref, k_ref, v_ref, lf_ref, qs_ref, ks_ref, vs_ref):
    tm = h_ref.shape[1]
    xn = _rmsnorm(h_ref[0], g_ref[...]).astype(BF16)
    qkvt = _dot_nt(wt_ref[...], xn)
    k_ref[0] = qkvt[D_MODEL:2 * D_MODEL]
    v_ref[0] = qkvt[2 * D_MODEL:]
    lf_ref[0] = _log_sigmoid(_dot_nt(wft_ref[...], xn) + bf_ref[...])

    row = lax.broadcasted_iota(jnp.int32, (HEAD_DIM, 1), 0)
    q_tail = jnp.broadcast_to(jnp.where(row < C_TERMS, -1.0, 0.0), (HEAD_DIM, tm)).astype(BF16)
    k_tail = jnp.zeros((HEAD_DIM, tm), BF16)
    v_tail = jnp.broadcast_to(jnp.where(row[:V_ROWS - HEAD_DIM] == 0, 1.0, 0.0), (V_ROWS - HEAD_DIM, tm)).astype(BF16)
    for h in range(N_HEADS):
        head = slice(h * HEAD_DIM, (h + 1) * HEAD_DIM)
        qs_ref[0, h, :HEAD_DIM] = (qkvt[:D_MODEL][head] * (HEAD_DIM ** -0.5 * LOG2E)).astype(BF16)
        qs_ref[0, h, HEAD_DIM:] = q_tail
        ks_ref[0, h, :HEAD_DIM] = qkvt[D_MODEL:2 * D_MODEL][head].astype(BF16)
        ks_ref[0, h, HEAD_DIM:] = k_tail
        vs_ref[0, h, :HEAD_DIM] = qkvt[2 * D_MODEL:][head].astype(BF16)
        vs_ref[0, h, HEAD_DIM:] = v_tail


def attn_proj_prompt(h, g, wt_bf, wft_bf, bf_col, *, tm):
    b, s, _ = h.shape
    col = pl.BlockSpec((1, D_MODEL, tm), lambda i, j: (i, 0, j))
    slab = pl.BlockSpec((1, N_HEADS, LANES, tm), lambda i, j: (i, 0, 0, j))
    return pl.pallas_call(
        _attn_proj_prompt_kernel,
        out_shape=(jax.ShapeDtypeStruct((b, D_MODEL, s), F32), jax.ShapeDtypeStruct((b, D_MODEL, s), F32),
                   jax.ShapeDtypeStruct((b, N_HEADS, s), F32),
                   jax.ShapeDtypeStruct((b, N_HEADS, LANES, s), BF16),
                   jax.ShapeDtypeStruct((b, N_HEADS, LANES, s), BF16),
                   jax.ShapeDtypeStruct((b, N_HEADS, V_ROWS, s), BF16)),
        grid=(b, s // tm),
        in_specs=[pl.BlockSpec((1, tm, D_MODEL), lambda i, j: (i, j, 0)),
                  _resident((1, D_MODEL)),
                  _resident((3 * D_MODEL, D_MODEL)),
                  _resident((N_HEADS, D_MODEL)),
                  _resident((N_HEADS, 1))],
        out_specs=(col, col,
                   pl.BlockSpec((1, N_HEADS, tm), lambda i, j: (i, 0, j)),
                   slab, slab,
                   pl.BlockSpec((1, N_HEADS, V_ROWS, tm), lambda i, j: (i, 0, 0, j))),
        compiler_params=_params("parallel", "parallel"),
        name="attn_proj_prompt",
    )(h, g, wt_bf, wft_bf, bf_col)


def _attn_proj_kernel(h_ref, g_ref, wq_ref, wkvt_ref, wft_ref, bf_ref, q_ref, k_ref, v_ref, lf_ref):
    xn = _rmsnorm(h_ref[0], g_ref[...]).astype(BF16)
    q_ref[0] = (_dot(xn, wq_ref[...]) * (HEAD_DIM ** -0.5)).astype(BF16)
    kv = _dot_nt(wkvt_ref[...], xn).T
    k_ref[0] = kv[:, :D_MODEL]
    v_ref[0] = kv[:, D_MODEL:]
    lf_ref[0] = _log_sigmoid(_dot_nt(wft_ref[...], xn) + bf_ref[...])


def attn_proj(h, g, wq_bf, wkvt_bf, wft_bf, bf_col, *, tm):
    b, s, _ = h.shape
    row = pl.BlockSpec((1, tm, D_MODEL), lambda i, j: (i, j, 0))
    out_shape = [jax.ShapeDtypeStruct((b, s, D_MODEL), BF16),
                 jax.ShapeDtypeStruct((b, s, D_MODEL), F32), jax.ShapeDtypeStruct((b, s, D_MODEL), F32),
                 jax.ShapeDtypeStruct((b, N_HEADS, s), F32)]
    out_specs = [row, row, row, pl.BlockSpec((1, N_HEADS, tm), lambda i, j: (i, 0, j))]
    return pl.pallas_call(
        _attn_proj_kernel,
        out_shape=tuple(out_shape),
        grid=(b, s // tm),
        in_specs=[row,
                  _resident((1, D_MODEL)),
                  _resident((D_MODEL, D_MODEL)),
                  _resident((2 * D_MODEL, D_MODEL)),
                  _resident((N_HEADS, D_MODEL)),
                  _resident((N_HEADS, 1))],
        out_specs=tuple(out_specs),
        compiler_params=_params("parallel", "parallel"),
        name="attn_proj",
    )(h, g, wq_bf, wkvt_bf, wft_bf, bf_col)


CUMSUM_CHUNK = 256


def _cumsum_kernel(x_ref, slab_hbm_ref, c_ref):
    del slab_hbm_ref
    s = x_ref.shape[2]
    upper = _upper_tri(CUMSUM_CHUNK)
    row = lax.broadcasted_iota(jnp.int32, (C_ROWS, CUMSUM_CHUNK), 0)
    carry = jnp.zeros((N_HEADS, 1), F32)
    for j in range(s // CUMSUM_CHUNK):
        sl = slice(j * CUMSUM_CHUNK, (j + 1) * CUMSUM_CHUNK)
        local = _dot_exact(x_ref[0, :, sl], upper)
        c = (local + carry) * LOG2E
        carry = carry + local[:, CUMSUM_CHUNK - 1:CUMSUM_CHUNK]
        hi = c.astype(BF16).astype(F32)
        mid = (c - hi).astype(BF16).astype(F32)
        lo = c - hi - mid
        for h in range(N_HEADS):
            rows = jnp.where(row == 0, hi[h:h + 1], jnp.where(row == 1, mid[h:h + 1],
                                                               jnp.where(row == 2, lo[h:h + 1], 0.0)))
            c_ref[0, h, :, sl] = rows.astype(BF16)


def cumsum_into_slabs(lft, k_slabs):
    b, h, s = lft.shape
    return pl.pallas_call(
        _cumsum_kernel,
        out_shape=jax.ShapeDtypeStruct(k_slabs.shape, k_slabs.dtype),
        grid=(b,),
        in_specs=[pl.BlockSpec((1, h, s), lambda i: (i, 0, 0)),
                  pl.BlockSpec(memory_space=pl.ANY)],
        out_specs=pl.BlockSpec((1, h, C_ROWS, s), lambda i: (i, 0, C_ROW // C_ROWS, 0)),
        input_output_aliases={1: 0},
        compiler_params=_params("parallel"),
        name="cumsum_into_slabs",
    )(lft, k_slabs)


def _flash_kernel(qi_ref, ki_ref, q_ref, k_ref, v_ref, o_ref, acc_ref, m_ref, *, tq, cb, kc, hpi):
    t = pl.program_id(1)
    qi = qi_ref[t]
    ki = ki_ref[t]

    @pl.when(ki == 0)
    def _():
        m_ref[...] = jnp.full(m_ref.shape, NEG, F32)
        acc_ref[...] = jnp.zeros(acc_ref.shape, F32)

    def sweep(diagonal):
        def heads(j, carry):
            chains = [(j * hpi + hh, c) for hh in range(hpi) for c in range(tq // cb)]
            cols = [slice(c * cb, (c + 1) * cb) for _, c in chains]
            nk = [(c + 1) * cb if diagonal else tq for _, c in chains]
            n = range(len(chains))
            m = [m_ref[chains[i][0], :, cols[i]] for i in n]
            acc = [acc_ref[chains[i][0], :, cols[i]] for i in n]
            for k0 in range(0, tq, kc):
                live = [i for i in n if k0 < nk[i]]
                s = {i: lax.dot_general(k_ref[0, chains[i][0], :, k0:k0 + kc], q_ref[0, chains[i][0], :, cols[i]],
                                        TN_DIMS, preferred_element_type=F32)
                     for i in live}
                for i in live:
                    si = s[i]
                    if diagonal and k0 + kc > chains[i][1] * cb:
                        kpos = lax.broadcasted_iota(jnp.int32, si.shape, 0) + k0
                        qpos = lax.broadcasted_iota(jnp.int32, si.shape, 1) + chains[i][1] * cb
                        si = jnp.where(kpos <= qpos, si, NEG)
                    m_new = jnp.maximum(m[i], jnp.max(si, axis=0, keepdims=True))
                    pr = jnp.exp2(si - m_new[:1])
                    acc[i] = (jnp.exp2(m[i] - m_new)[:1] * acc[i]
                              + _dot(v_ref[0, chains[i][0], :, k0:k0 + kc], pr.astype(BF16)))
                    m[i] = m_new
            for i in n:
                acc_ref[chains[i][0], :, cols[i]] = acc[i]
                m_ref[chains[i][0], :, cols[i]] = m[i]
            return carry

        lax.fori_loop(0, N_HEADS // hpi, heads, 0)

    @pl.when(ki < qi)
    def _():
        sweep(False)

    @pl.when(ki == qi)
    def _():
        sweep(True)
        for p in range(N_PAIRS):
            pair = [acc_ref[2 * p + hh] for hh in range(2)]
            pair = [a[:HEAD_DIM] / a[L_ROW:L_ROW + 1] for a in pair]
            o_ref[0, :, p * LANES:(p + 1) * LANES] = jnp.concatenate(pair, axis=0).T.astype(o_ref.dtype)


def flash_prompt(q_slabs, k_slabs, v_slabs, *, tq, cb, kc, hpi):
    b, _, _, s = q_slabs.shape
    nq = s // tq
    qi = jnp.asarray([i for i in range(nq) for _ in range(i + 1)], jnp.int32)
    ki = jnp.asarray([j for i in range(nq) for j in range(i + 1)], jnp.int32)
    grid_spec = pltpu.PrefetchScalarGridSpec(
        num_scalar_prefetch=2,
        grid=(b, int(qi.shape[0])),
        in_specs=[pl.BlockSpec((1, N_HEADS, LANES, tq), lambda i, t, qi, ki: (i, 0, 0, qi[t])),
                  pl.BlockSpec((1, N_HEADS, LANES, tq), lambda i, t, qi, ki: (i, 0, 0, ki[t])),
                  pl.BlockSpec((1, N_HEADS, V_ROWS, tq), lambda i, t, qi, ki: (i, 0, 0, ki[t]))],
        out_specs=pl.BlockSpec((1, tq, D_MODEL), lambda i, t, qi, ki: (i, qi[t], 0)),
        scratch_shapes=[pltpu.VMEM((N_HEADS, V_ROWS, tq), F32),
                        pltpu.VMEM((N_HEADS, SUBLANES, tq), F32)],
    )
    return pl.pallas_call(
        functools.partial(_flash_kernel, tq=tq, cb=cb, kc=kc, hpi=hpi),
        out_shape=jax.ShapeDtypeStruct((b, s, D_MODEL), BF16),
        grid_spec=grid_spec,
        compiler_params=_params("parallel", "arbitrary"),
        name="flash_prompt",
    )(qi, ki, q_slabs, k_slabs, v_slabs)


NEW_COLS = 16


def _sample_attn_step(g, last, q_ref, kn_ref, vn_ref, lfn_ref, kt_refs, vt_refs, lp_refs,
                      o_ref, qbd_ref, acc_ref, m_ref, l_ref, carry_ref):
    dec_seq = q_ref.shape[1]
    n_rows = dec_seq * N_HEADS

    lane_d = lax.broadcasted_iota(jnp.int32, (N_HEADS, D_MODEL), 1)
    head_d = lax.broadcasted_iota(jnp.int32, (N_HEADS, D_MODEL), 0)
    head_mask = (lane_d // HEAD_DIM == head_d).astype(F32)

    @pl.when(g == 0)
    def _():
        qf = q_ref[0].astype(F32)
        for t in range(dec_seq):
            qbd_ref[t * N_HEADS:(t + 1) * N_HEADS, :] = (head_mask * qf[t:t + 1, :]).astype(BF16)
        m_ref[...] = jnp.full(m_ref.shape, NEG, F32)
        l_ref[...] = jnp.zeros(l_ref.shape, F32)
        acc_ref[...] = jnp.zeros(acc_ref.shape, F32)
        carry_ref[...] = jnp.zeros(carry_ref.shape, F32)

    def attend(scores, lf_rows, mask, pvs):
        n = lf_rows[0].shape[1]
        local_all = _dot_exact(jnp.concatenate(lf_rows, axis=0), _upper_tri(n))
        carry = carry_ref[...]
        m_prev = m_ref[...]
        m_new = m_prev
        s = []
        for i, sc in enumerate(scores):
            local = jnp.concatenate([local_all[i * N_HEADS:(i + 1) * N_HEADS]] * dec_seq, axis=0)
            si = sc - (local + carry[:, :n])
            carry = carry + local[:, n - 1:n]
            if mask is not None:
                si = jnp.where(mask, si, NEG)
            m_new = jnp.maximum(m_new, jnp.max(si, axis=1, keepdims=True))
            s.append(si)
        alpha = jnp.exp(m_prev - m_new)
        l_new = alpha * l_ref[...]
        acc = alpha[:, :1] * acc_ref[...]
        for si, pv in zip(s, pvs):
            pr = jnp.exp(si - m_new[:, :n])
            l_new = l_new + jnp.sum(pr, axis=1, keepdims=True)
            acc = acc + pv(pr.astype(BF16))
        carry_ref[...] = carry
        m_ref[...] = m_new
        l_ref[...] = l_new
        acc_ref[...] = acc

    qbd = qbd_ref[...]
    attend([_dot(qbd, kt[...].astype(BF16)) for kt in kt_refs], [lp[...] for lp in lp_refs], None,
           [lambda pr, vt=vt: _dot_nt(pr, vt[...].astype(BF16)) for vt in vt_refs])

    @pl.when(last)
    def _():
        r = lax.broadcasted_iota(jnp.int32, (n_rows, NEW_COLS), 0)
        col = lax.broadcasted_iota(jnp.int32, (n_rows, NEW_COLS), 1)
        attend([_dot_nt(qbd_ref[...], kn_ref[0].astype(BF16))], [lfn_ref[0]], col <= r // N_HEADS,
               [lambda pr: _dot(pr, vn_ref[0].astype(BF16))])
        o = acc_ref[...] / l_ref[:, :1]
        rows = [jnp.sum(o[t * N_HEADS:(t + 1) * N_HEADS, :] * head_mask, axis=0, keepdims=True)
                for t in range(dec_seq)]
        o_ref[0] = jnp.concatenate(rows, axis=0)


def _sample_attn_conv_kernel(pt_ref, q_ref, kn_ref, vn_ref, lfn_ref, *refs, group, conv_tiles):
    kt_refs, vt_refs, lp_refs = refs[:group], refs[group:2 * group], refs[2 * group:3 * group]
    conv_in_refs = refs[3 * group:3 * group + 5]
    o_ref, z_ref, st_ref = refs[3 * group + 5:3 * group + 8]
    attn_scratch = refs[3 * group + 8:3 * group + 13]
    conv_scratch = refs[3 * group + 13:]
    g = pl.program_id(1)
    step = pl.program_id(0) * pl.num_programs(1) + g
    _sample_attn_step(g, g == pl.num_programs(1) - 1, q_ref, kn_ref, vn_ref, lfn_ref, kt_refs, vt_refs, lp_refs,
                      o_ref, *attn_scratch)
    _conv_mid_step(step % conv_tiles, *conv_in_refs, z_ref, st_ref, *conv_scratch)


def sample_attention_and_conv_mid(page_ids, q, k_new, v_new, lft_new, cache_kt, cache_vt, cache_lft,
                                  u, w_dw, b_dw, ln_g, ln_b, *, group):
    db, n_pages = page_ids.shape
    dec_seq = q.shape[1]
    n_rows = dec_seq * N_HEADS
    n_groups = n_pages // group
    bsz, seq, _ = u.shape
    tt = bsz * seq // (db * n_groups)
    conv_tiles = seq // tt
    assert tt * db * n_groups == bsz * seq and conv_tiles * tt == seq and tt % SUBLANES == 0 and tt >= HIST_PAD

    def page_spec(rows, i):
        return pl.BlockSpec((None, rows, PAGE), lambda b, g, pt: (pt[b, g * group + i], 0, 0))

    def per_batch(rows, width):
        return pl.BlockSpec((1, rows, width), lambda b, g, pt: (b, 0, 0))

    conv_tile = pl.BlockSpec((1, tt, D_MODEL), lambda b, g, pt: ((b * n_groups + g) // conv_tiles,
                                                                  (b * n_groups + g) % conv_tiles, 0))
    grid_spec = pltpu.PrefetchScalarGridSpec(
        num_scalar_prefetch=1,
        grid=(db, n_groups),
        in_specs=([per_batch(dec_seq, D_MODEL), per_batch(NEW_COLS, D_MODEL), per_batch(NEW_COLS, D_MODEL),
                   per_batch(N_HEADS, NEW_COLS)]
                  + [page_spec(D_MODEL, i) for i in range(group)]
                  + [page_spec(D_MODEL, i) for i in range(group)]
                  + [page_spec(N_HEADS, i) for i in range(group)]
                  + [conv_tile, _resident((CONV_WIDTH, D_MODEL)), _resident((1, D_MODEL)),
                     _resident((1, D_MODEL)), _resident((1, D_MODEL))]),
        out_specs=(per_batch(dec_seq, D_MODEL), conv_tile,
                   pl.BlockSpec((1, CONV_HIST, D_MODEL), lambda b, g, pt: ((b * n_groups + g) // conv_tiles, 0, 0))),
        scratch_shapes=[pltpu.VMEM((n_rows, D_MODEL), BF16),
                        pltpu.VMEM((n_rows, D_MODEL), F32),
                        pltpu.VMEM((n_rows, LANES), F32),
                        pltpu.VMEM((n_rows, LANES), F32),
                        pltpu.VMEM((n_rows, LANES), F32),
                        pltpu.VMEM((SUBLANES, HIST_PAD + tt, D_MODEL), F32),
                        pltpu.VMEM((tt, D_MODEL), F32)],
    )
    return pl.pallas_call(
        functools.partial(_sample_attn_conv_kernel, group=group, conv_tiles=conv_tiles),
        out_shape=(jax.ShapeDtypeStruct((db, dec_seq, D_MODEL), F32),
                   jax.ShapeDtypeStruct((bsz, seq, D_MODEL), BF16),
                   jax.ShapeDtypeStruct((bsz, CONV_HIST, D_MODEL), F32)),
        grid_spec=grid_spec,
        compiler_params=_params("arbitrary", "arbitrary"),
        name="sample_attention_and_conv_mid",
    )(page_ids, q, k_new, v_new, lft_new, *([cache_kt] * group), *([cache_vt] * group), *([cache_lft] * group),
      u, w_dw, b_dw, ln_g, ln_b)


PROMPT_TM = 512
FFN_CHUNK = 2816
ATTN_TQ = 512
ATTN_CB = 256
ATTN_KC = 256
ATTN_HPI = 16
PAGE_GROUP = 16


def kernel(x_prompt, x_sample, state_conv, cache_k, cache_v, cache_logf, page_table, norm_mix_g, norm_ffn_g,
           norm_final_g, conv_w_in, conv_b_in, conv_w_dw, conv_b_dw, conv_ln_g, conv_ln_b, conv_w_out, conv_b_out,
           attn_w_in, attn_b_f, attn_w_out, ffn_w_gu, ffn_w_down):
    bsz, seq, d = x_prompt.shape
    dbsz, dseq, _ = x_sample.shape
    n_sample = dbsz * dseq
    row = lambda a: a.reshape(1, -1)
    to_time_major = lambda a: jnp.transpose(a.reshape(dbsz, dseq, -1), (1, 0, 2)).reshape(n_sample, -1)
    to_batch_major = lambda a: jnp.transpose(a.reshape(dseq, dbsz, -1), (1, 0, 2))

    w_in = conv_w_in[0].astype(BF16)
    w_out = conv_w_out[0].astype(BF16)
    wgu = ffn_w_gu.astype(BF16)
    wd = ffn_w_down.astype(BF16)
    conv_args = (conv_w_dw[0], row(conv_b_dw[0]), row(conv_ln_g[0]), row(conv_ln_b[0]))
    w_in_t = jnp.transpose(attn_w_in[0]).astype(BF16)
    wq = attn_w_in[0, :, :d].astype(BF16)
    wft = w_in_t[3 * d:]
    bf_col = attn_b_f[0].reshape(N_HEADS, 1)
    wo1 = attn_w_out[0].astype(BF16)

    hp = x_prompt.reshape(bsz * seq, d)
    hs = to_time_major(x_sample)

    us = conv_in(hs, row(norm_mix_g[0]), w_in, row(conv_b_in[0]), tm=n_sample)
    zs, conv_state_s = conv_mid_sample(us.reshape(dseq, dbsz, d), jnp.transpose(state_conv[0], (1, 0, 2)),
                                       *conv_args)
    hs = mix_ffn(hs, zs.reshape(n_sample, d), w_out, row(conv_b_out[0]), row(norm_ffn_g[0]), wgu, wd, None,
                 layer=0, tm=n_sample, fc=FFN_CHUNK)
    qs, ks, vs, lfts = attn_proj(hs.reshape(1, n_sample, d), row(norm_mix_g[1]), wq, w_in_t[d:3 * d], wft, bf_col,
                                 tm=n_sample)
    ks_b = to_batch_major(ks[0])
    vs_b = to_batch_major(vs[0])
    lfts_b = jnp.transpose(lfts.reshape(N_HEADS, dseq, dbsz), (2, 0, 1))
    pad_rows = lambda a: jnp.pad(a, ((0, 0), (0, NEW_COLS - dseq), (0, 0)))
    cache_kt = jnp.transpose(cache_k, (0, 1, 3, 4, 2)).reshape(-1, d, PAGE)
    cache_vt = jnp.transpose(cache_v, (0, 1, 3, 4, 2)).reshape(-1, d, PAGE)
    cache_lft = jnp.transpose(cache_logf, (0, 1, 3, 2)).reshape(-1, N_HEADS, PAGE)

    up = conv_in(hp, row(norm_mix_g[0]), w_in, row(conv_b_in[0]), tm=PROMPT_TM)
    os_, zp, conv_state_p = sample_attention_and_conv_mid(
        page_table, to_batch_major(qs[0]), pad_rows(ks_b), pad_rows(vs_b),
        jnp.pad(lfts_b, ((0, 0), (0, 0), (0, NEW_COLS - dseq))), cache_kt, cache_vt, cache_lft,
        up.reshape(bsz, seq, d), *conv_args, group=PAGE_GROUP)
    hp = mix_ffn(hp, zp.reshape(bsz * seq, d), w_out, row(conv_b_out[0]), row(norm_ffn_g[0]), wgu, wd, None,
                 layer=0, tm=PROMPT_TM, fc=FFN_CHUNK)

    ktp, vtp, lftp, q_slabs, k_slabs, v_slabs = attn_proj_prompt(
        hp.reshape(bsz, seq, d), row(norm_mix_g[1]), w_in_t[:3 * d], wft, bf_col, tm=PROMPT_TM)
    k_slabs = cumsum_into_slabs(lftp, k_slabs)
    op = flash_prompt(q_slabs, k_slabs, v_slabs, tq=ATTN_TQ, cb=ATTN_CB, kc=ATTN_KC, hpi=ATTN_HPI)
    hp = mix_ffn(hp, op.reshape(bsz * seq, d), wo1, None, row(norm_ffn_g[1]), wgu, wd, row(norm_final_g),
                 layer=1, tm=PROMPT_TM, fc=FFN_CHUNK)
    hs = mix_ffn(hs, to_time_major(os_), wo1, None, row(norm_ffn_g[1]), wgu, wd, row(norm_final_g),
                 layer=1, tm=n_sample, fc=FFN_CHUNK)

    heads_t = lambda a: jnp.transpose(a.reshape(bsz, N_HEADS, HEAD_DIM, seq), (0, 3, 1, 2))[None]
    heads = lambda a: a.reshape(1, dbsz, dseq, N_HEADS, HEAD_DIM)
    return (hp.reshape(bsz, seq, d), to_batch_major(hs),
            conv_state_p[None], jnp.transpose(conv_state_s, (1, 0, 2))[None],
            heads_t(ktp), heads_t(vtp), jnp.transpose(lftp, (0, 2, 1))[None],
            heads(ks_b), heads(vs_b), jnp.transpose(lfts_b, (0, 2, 1))[None])
```

```python
import functools

import jax
import jax.numpy as jnp
from jax import lax
from jax.experimental import pallas as pl
from jax.experimental.pallas import tpu as pltpu

D_MODEL = 1024
N_HEADS = 16
HEAD_DIM = 64
N_PAIRS = N_HEADS // 2
LANES = 128
D_FF = 2816
CONV_WIDTH = 31
CONV_HIST = CONV_WIDTH - 1
HIST_PAD = 32
EPS = 1e-6
PAGE = 128
NEG = -1e30
VMEM_LIMIT_BYTES = 56 * 1024 * 1024

F32 = jnp.float32
BF16 = jnp.bfloat16
NT_DIMS = (((1,), (1,)), ((), ()))
TN_DIMS = (((0,), (0,)), ((), ()))


def _params(*sem):
    return pltpu.CompilerParams(dimension_semantics=sem, vmem_limit_bytes=VMEM_LIMIT_BYTES)


def _resident(shape, layer=None):
    nd = len(shape)
    if layer is None:
        return pl.BlockSpec(shape, lambda *_: (0,) * nd, pipeline_mode=pl.Buffered(1))
    return pl.BlockSpec((None,) + shape, lambda *_: (layer,) + (0,) * nd, pipeline_mode=pl.Buffered(1))


def _rmsnorm(x, g):
    return x * lax.rsqrt(jnp.mean(x * x, axis=-1, keepdims=True) + EPS) * g


def _sigmoid(x):
    return 1.0 / (1.0 + jnp.exp(-x))


def _log_sigmoid(x):
    return jnp.minimum(x, 0.0) - jnp.log1p(jnp.exp(-jnp.abs(x)))


def _layernorm_silu(y, g, b):
    mu = jnp.mean(y, axis=-1, keepdims=True)
    yc = y - mu
    var = jnp.mean(yc * yc, axis=-1, keepdims=True)
    yn = yc * lax.rsqrt(var + EPS) * g + b
    return yn * _sigmoid(yn)


def _dot(a, b):
    return jnp.dot(a, b, preferred_element_type=F32)


def _dot_nt(a, b):
    return lax.dot_general(a, b, NT_DIMS, preferred_element_type=F32)


def _dot_exact(a, b):
    return jnp.dot(a, b, preferred_element_type=F32, precision=lax.Precision.HIGHEST)


def _upper_tri(n):
    return (lax.broadcasted_iota(jnp.int32, (n, n), 0) <= lax.broadcasted_iota(jnp.int32, (n, n), 1)).astype(F32)


def _conv_in_kernel(h_ref, g_ref, w_ref, b_ref, u_ref):
    xn = _rmsnorm(h_ref[...], g_ref[...]).astype(BF16)
    hh = _dot(xn, w_ref[...]) + b_ref[...]
    u_ref[...] = hh[:, :D_MODEL] * _sigmoid(hh[:, D_MODEL:])


def conv_in(h, g, w_bf, b, *, tm):
    m = h.shape[0]
    return pl.pallas_call(
        _conv_in_kernel,
        out_shape=jax.ShapeDtypeStruct((m, D_MODEL), F32),
        grid=(m // tm,),
        in_specs=[pl.BlockSpec((tm, D_MODEL), lambda i: (i, 0)),
                  _resident((1, D_MODEL)),
                  _resident((D_MODEL, 2 * D_MODEL)),
                  _resident((1, 2 * D_MODEL))],
        out_specs=pl.BlockSpec((tm, D_MODEL), lambda i: (i, 0)),
        compiler_params=_params("parallel"),
        name="conv_in",
    )(h, g, w_bf, b)


SUBLANES = 8
CONV_RC = 16


def _conv_mid_step(t, u_ref, wdw_ref, bdw_ref, lng_ref, lnb_ref, z_ref, st_ref, ubuf, ybuf):
    tt = u_ref.shape[1]
    n_shift = ubuf.shape[1]

    @pl.when(t == 0)
    def _():
        ubuf[0, 0:HIST_PAD, :] = jnp.zeros((HIST_PAD, D_MODEL), F32)

    @pl.when(t > 0)
    def _():
        ubuf[0, 0:HIST_PAD, :] = ubuf[0, tt:tt + HIST_PAD, :]

    ubuf[0, HIST_PAD:HIST_PAD + tt, :] = u_ref[0]
    st_ref[0] = ubuf[0, tt + HIST_PAD - CONV_HIST:tt + HIST_PAD, :]
    for s in range(1, SUBLANES):
        ubuf[s, 0:n_shift - SUBLANES, :] = ubuf[0, s:s + n_shift - SUBLANES, :]

    off = HIST_PAD - CONV_HIST
    rc = min(tt, CONV_RC)

    def row_chunk(i, carry):
        r0 = pl.multiple_of(i * rc, rc)
        for l0 in range(0, D_MODEL, LANES):
            acc = jnp.broadcast_to(bdw_ref[:, l0:l0 + LANES], (rc, LANES))
            for j in range(CONV_WIDTH):
                shift, base = (j + off) % SUBLANES, (j + off) // SUBLANES * SUBLANES
                acc = acc + wdw_ref[j:j + 1, l0:l0 + LANES] * ubuf[shift, pl.ds(r0 + base, rc), l0:l0 + LANES]
            ybuf[pl.ds(r0, rc), l0:l0 + LANES] = acc
        return carry

    lax.fori_loop(0, tt // rc, row_chunk, 0)

    z_ref[0] = _layernorm_silu(ybuf[...], lng_ref[...], lnb_ref[...]).astype(z_ref.dtype)


def _conv_mid_sample_kernel(u_ref, hist_ref, wdw_ref, bdw_ref, lng_ref, lnb_ref, z_ref, st_ref, upad):
    dseq = u_ref.shape[0]
    upad[0:CONV_HIST] = hist_ref[...]
    upad[CONV_HIST:CONV_HIST + dseq] = u_ref[...]
    st_ref[...] = upad[dseq:dseq + CONV_HIST]
    for t in range(dseq):
        acc = jnp.broadcast_to(bdw_ref[...], upad.shape[1:])
        for j in range(CONV_WIDTH):
            acc = acc + wdw_ref[j:j + 1, :] * upad[t + j]
        z_ref[t] = _layernorm_silu(acc, lng_ref[...], lnb_ref[...])


def conv_mid_sample(u, hist, w_dw, b_dw, ln_g, ln_b):
    dseq, db, _ = u.shape
    whole = lambda shape: pl.BlockSpec(shape, lambda i: (0,) * len(shape))
    return pl.pallas_call(
        _conv_mid_sample_kernel,
        out_shape=(jax.ShapeDtypeStruct((dseq, db, D_MODEL), F32),
                   jax.ShapeDtypeStruct((CONV_HIST, db, D_MODEL), F32)),
        grid=(1,),
        in_specs=[whole(u.shape), whole(hist.shape), whole(w_dw.shape), whole(b_dw.shape), whole(ln_g.shape),
                  whole(ln_b.shape)],
        out_specs=(whole((dseq, db, D_MODEL)), whole((CONV_HIST, db, D_MODEL))),
        scratch_shapes=[pltpu.VMEM((CONV_HIST + dseq, db, D_MODEL), F32)],
        compiler_params=_params("arbitrary"),
        name="conv_mid_sample",
    )(u, hist, w_dw, b_dw, ln_g, ln_b)


def _mix_ffn_kernel(*refs, fc, has_bias, final):
    h_ref, z_ref, wo_ref = refs[:3]
    refs = refs[3:]
    if has_bias:
        bo_ref, refs = refs[0], refs[1:]
    g_ref, wgu_ref, wd_ref = refs[:3]
    refs = refs[3:]
    if final:
        gf_ref, refs = refs[0], refs[1:]
    out_ref = refs[0]

    h1 = h_ref[...] + _dot(z_ref[...].astype(BF16), wo_ref[...])
    if has_bias:
        h1 = h1 + bo_ref[...]
    xn = _rmsnorm(h1, g_ref[...]).astype(BF16)
    acc = h1
    for c in range(D_FF // fc):
        gate = _dot(xn, wgu_ref[:, c * fc:(c + 1) * fc])
        up = _dot(xn, wgu_ref[:, D_FF + c * fc:D_FF + (c + 1) * fc])
        a = (gate * _sigmoid(gate) * up).astype(BF16)
        acc = acc + _dot(a, wd_ref[c * fc:(c + 1) * fc, :])
    out_ref[...] = _rmsnorm(acc, gf_ref[...]) if final else acc


def mix_ffn(h, z, wo_bf, bo, g_ffn, wgu_bf, wd_bf, g_final, *, layer, tm, fc):
    m = h.shape[0]
    has_bias = bo is not None
    final = g_final is not None
    row = pl.BlockSpec((tm, D_MODEL), lambda i: (i, 0))
    args = [h, z, wo_bf]
    specs = [row, row, _resident((D_MODEL, D_MODEL))]
    if has_bias:
        args.append(bo)
        specs.append(_resident((1, D_MODEL)))
    args += [g_ffn, wgu_bf, wd_bf]
    specs += [_resident((1, D_MODEL)), _resident((D_MODEL, 2 * D_FF), layer), _resident((D_FF, D_MODEL), layer)]
    if final:
        args.append(g_final)
        specs.append(_resident((1, D_MODEL)))
    return pl.pallas_call(
        functools.partial(_mix_ffn_kernel, fc=fc, has_bias=has_bias, final=final),
        out_shape=jax.ShapeDtypeStruct((m, D_MODEL), F32),
        grid=(m // tm,),
        in_specs=specs,
        out_specs=row,
        compiler_params=_params("parallel"),
        name="mix_ffn",
    )(*args)


C_ROW = HEAD_DIM
C_TERMS = 3
C_ROWS = 16
L_ROW = HEAD_DIM
V_ROWS = HEAD_DIM + 16
LOG2E = 1.4426950408889634


def _attn_proj_prompt_kernel(h_ref, g_ref, wt_ref, wft_ref, bf_ref, k_ref, v_ref, lf_ref, qs_ref, ks_ref, vs_ref):
    tm = h_ref.shape[1]
    xn = _rmsnorm(h_ref[0], g_ref[...]).astype(BF16)
    qkvt = _dot_nt(wt_ref[...], xn)
    k_ref[0] = qkvt[D_MODEL:2 * D_MODEL]
    v_ref[0] = qkvt[2 * D_MODEL:]
    lf_ref[0] = _log_sigmoid(_dot_nt(wft_ref[...], xn) + bf_ref[...])

    row = lax.broadcasted_iota(jnp.int32, (HEAD_DIM, 1), 0)
    q_tail = jnp.broadcast_to(jnp.where(row < C_TERMS, -1.0, 0.0), (HEAD_DIM, tm)).astype(BF16)
    k_tail = jnp.zeros((HEAD_DIM, tm), BF16)
    v_tail = jnp.broadcast_to(jnp.where(row[:V_ROWS - HEAD_DIM] == 0, 1.0, 0.0), (V_ROWS - HEAD_DIM, tm)).astype(BF16)
    for h in range(N_HEADS):
        head = slice(h * HEAD_DIM, (h + 1) * HEAD_DIM)
        qs_ref[0, h, :HEAD_DIM] = (qkvt[:D_MODEL][head] * (HEAD_DIM ** -0.5 * LOG2E)).astype(BF16)
        qs_ref[0, h, HEAD_DIM:] = q_tail
        ks_ref[0, h, :HEAD_DIM] = qkvt[D_MODEL:2 * D_MODEL][head].astype(BF16)
        ks_ref[0, h, HEAD_DIM:] = k_tail
        vs_ref[0, h, :HEAD_DIM] = qkvt[2 * D_MODEL:][head].astype(BF16)
        vs_ref[0, h, HEAD_DIM:] = v_tail


def attn_proj_prompt(h, g, wt_bf, wft_bf, bf_col, *, tm):
    b, s, _ = h.shape
    col = pl.BlockSpec((1, D_MODEL, tm), lambda i, j: (i, 0, j))
    slab = pl.BlockSpec((1, N_HEADS, LANES, tm), lambda i, j: (i, 0, 0, j))
    return pl.pallas_call(
        _attn_proj_prompt_kernel,
        out_shape=(jax.ShapeDtypeStruct((b, D_MODEL, s), F32), jax.ShapeDtypeStruct((b, D_MODEL, s), F32),
                   jax.ShapeDtypeStruct((b, N_HEADS, s), F32),
                   jax.ShapeDtypeStruct((b, N_HEADS, LANES, s), BF16),
                   jax.ShapeDtypeStruct((b, N_HEADS, LANES, s), BF16),
                   jax.ShapeDtypeStruct((b, N_HEADS, V_ROWS, s), BF16)),
        grid=(b, s // tm),
        in_specs=[pl.BlockSpec((1, tm, D_MODEL), lambda i, j: (i, j, 0)),
                  _resident((1, D_MODEL)),
                  _resident((3 * D_MODEL, D_MODEL)),
                  _resident((N_HEADS, D_MODEL)),
                  _resident((N_HEADS, 1))],
        out_specs=(col, col,
                   pl.BlockSpec((1, N_HEADS, tm), lambda i, j: (i, 0, j)),
                   slab, slab,
                   pl.BlockSpec((1, N_HEADS, V_ROWS, tm), lambda i, j: (i, 0, 0, j))),
        compiler_params=_params("parallel", "parallel"),
        name="attn_proj_prompt",
    )(h, g, wt_bf, wft_bf, bf_col)


def _attn_proj_kernel(h_ref, g_ref, wq_ref, wkvt_ref, wft_ref, bf_ref, q_ref, k_ref, v_ref, lf_ref):
    xn = _rmsnorm(h_ref[0], g_ref[...]).astype(BF16)
    q_ref[0] = (_dot(xn, wq_ref[...]) * (HEAD_DIM ** -0.5)).astype(BF16)
    kv = _dot_nt(wkvt_ref[...], xn).T
    k_ref[0] = kv[:, :D_MODEL]
    v_ref[0] = kv[:, D_MODEL:]
    lf_ref[0] = _log_sigmoid(_dot_nt(wft_ref[...], xn) + bf_ref[...])


def attn_proj(h, g, wq_bf, wkvt_bf, wft_bf, bf_col, *, tm):
    b, s, _ = h.shape
    row = pl.BlockSpec((1, tm, D_MODEL), lambda i, j: (i, j, 0))
    out_shape = [jax.ShapeDtypeStruct((b, s, D_MODEL), BF16),
                 jax.ShapeDtypeStruct((b, s, D_MODEL), F32), jax.ShapeDtypeStruct((b, s, D_MODEL), F32),
                 jax.ShapeDtypeStruct((b, N_HEADS, s), F32)]
    out_specs = [row, row, row, pl.BlockSpec((1, N_HEADS, tm), lambda i, j: (i, 0, j))]
    return pl.pallas_call(
        _attn_proj_kernel,
        out_shape=tuple(out_shape),
        grid=(b, s // tm),
        in_specs=[row,
                  _resident((1, D_MODEL)),
                  _resident((D_MODEL, D_MODEL)),
                  _resident((2 * D_MODEL, D_MODEL)),
                  _resident((N_HEADS, D_MODEL)),
                  _resident((N_HEADS, 1))],
        out_specs=tuple(out_specs),
        compiler_params=_params("parallel", "parallel"),
        name="attn_proj",
    )(h, g, wq_bf, wkvt_bf, wft_bf, bf_col)


CUMSUM_CHUNK = 256


def _cumsum_kernel(x_ref, slab_hbm_ref, c_ref):
    del slab_hbm_ref
    s = x_ref.shape[2]
    upper = _upper_tri(CUMSUM_CHUNK)
    row = lax.broadcasted_iota(jnp.int32, (C_ROWS, CUMSUM_CHUNK), 0)
    carry = jnp.zeros((N_HEADS, 1), F32)
    for j in range(s // CUMSUM_CHUNK):
        sl = slice(j * CUMSUM_CHUNK, (j + 1) * CUMSUM_CHUNK)
        local = _dot_exact(x_ref[0, :, sl], upper)
        c = (local + carry) * LOG2E
        carry = carry + local[:, CUMSUM_CHUNK - 1:CUMSUM_CHUNK]
        hi = c.astype(BF16).astype(F32)
        mid = (c - hi).astype(BF16).astype(F32)
        lo = c - hi - mid
        for h in range(N_HEADS):
            rows = jnp.where(row == 0, hi[h:h + 1], jnp.where(row == 1, mid[h:h + 1],
                                                               jnp.where(row == 2, lo[h:h + 1], 0.0)))
            c_ref[0, h, :, sl] = rows.astype(BF16)


def cumsum_into_slabs(lft, k_slabs):
    b, h, s = lft.shape
    return pl.pallas_call(
        _cumsum_kernel,
        out_shape=jax.ShapeDtypeStruct(k_slabs.shape, k_slabs.dtype),
        grid=(b,),
        in_specs=[pl.BlockSpec((1, h, s), lambda i: (i, 0, 0)),
                  pl.BlockSpec(memory_space=pl.ANY)],
        out_specs=pl.BlockSpec((1, h, C_ROWS, s), lambda i: (i, 0, C_ROW // C_ROWS, 0)),
        input_output_aliases={1: 0},
        compiler_params=_params("parallel"),
        name="cumsum_into_slabs",
    )(lft, k_slabs)


def _flash_kernel(qi_ref, ki_ref, q_ref, k_ref, v_ref, o_ref, acc_ref, m_ref, *, tq, cb, kc, hpi):
    t = pl.program_id(1)
    qi = qi_ref[t]
    ki = ki_ref[t]

    @pl.when(ki == 0)
    def _():
        m_ref[...] = jnp.full(m_ref.shape, NEG, F32)
        acc_ref[...] = jnp.zeros(acc_ref.shape, F32)

    def sweep(diagonal):
        def heads(j, carry):
            chains = [(j * hpi + hh, c) for hh in range(hpi) for c in range(tq // cb)]
            cols = [slice(c * cb, (c + 1) * cb) for _, c in chains]
            nk = [(c + 1) * cb if diagonal else tq for _, c in chains]
            n = range(len(chains))
            for k0 in range(0, tq, kc):
                live = [i for i in n if k0 < nk[i]]
                s = {i: lax.dot_general(k_ref[0, chains[i][0], :, k0:k0 + kc], q_ref[0, chains[i][0], :, cols[i]],
                                        TN_DIMS, preferred_element_type=F32)
                     for i in live}
                for i in live:
                    si = s[i]
                    if diagonal and k0 + kc > chains[i][1] * cb:
                        kpos = lax.broadcasted_iota(jnp.int32, si.shape, 0) + k0
                        qpos = lax.broadcasted_iota(jnp.int32, si.shape, 1) + chains[i][1] * cb
                        si = jnp.where(kpos <= qpos, si, NEG)
                    h = chains[i][0]
                    m_prev = m_ref[h, :, cols[i]]
                    m_new = jnp.maximum(m_prev, jnp.max(si, axis=0, keepdims=True))
                    pr = jnp.exp2(si - m_new[:1])
                    acc_ref[h, :, cols[i]] = (jnp.exp2(m_prev - m_new)[:1] * acc_ref[h, :, cols[i]]
                                              + _dot(v_ref[0, h, :, k0:k0 + kc], pr.astype(BF16)))
                    m_ref[h, :, cols[i]] = m_new
            return carry

        lax.fori_loop(0, N_HEADS // hpi, heads, 0)

    @pl.when(ki < qi)
    def _():
        sweep(False)

    @pl.when(ki == qi)
    def _():
        sweep(True)
        for p in range(N_PAIRS):
            pair = [acc_ref[2 * p + hh] for hh in range(2)]
            pair = [a[:HEAD_DIM] / a[L_ROW:L_ROW + 1] for a in pair]
            o_ref[0, :, p * LANES:(p + 1) * LANES] = jnp.concatenate(pair, axis=0).T.astype(o_ref.dtype)


def flash_prompt(q_slabs, k_slabs, v_slabs, *, tq, cb, kc, hpi):
    b, _, _, s = q_slabs.shape
    nq = s // tq
    qi = jnp.asarray([i for i in range(nq) for _ in range(i + 1)], jnp.int32)
    ki = jnp.asarray([j for i in range(nq) for j in range(i + 1)], jnp.int32)
    grid_spec = pltpu.PrefetchScalarGridSpec(
        num_scalar_prefetch=2,
        grid=(b, int(qi.shape[0])),
        in_specs=[pl.BlockSpec((1, N_HEADS, LANES, tq), lambda i, t, qi, ki: (i, 0, 0, qi[t])),
                  pl.BlockSpec((1, N_HEADS, LANES, tq), lambda i, t, qi, ki: (i, 0, 0, ki[t])),
                  pl.BlockSpec((1, N_HEADS, V_ROWS, tq), lambda i, t, qi, ki: (i, 0, 0, ki[t]))],
        out_specs=pl.BlockSpec((1, tq, D_MODEL), lambda i, t, qi, ki: (i, qi[t], 0)),
        scratch_shapes=[pltpu.VMEM((N_HEADS, V_ROWS, tq), F32),
                        pltpu.VMEM((N_HEADS, SUBLANES, tq), F32)],
    )
    return pl.pallas_call(
        functools.partial(_flash_kernel, tq=tq, cb=cb, kc=kc, hpi=hpi),
        out_shape=jax.ShapeDtypeStruct((b, s, D_MODEL), BF16),
        grid_spec=grid_spec,
        compiler_params=_params("parallel", "arbitrary"),
        name="flash_prompt",
    )(qi, ki, q_slabs, k_slabs, v_slabs)


NEW_COLS = 16


def _sample_attn_step(g, last, q_ref, kn_ref, vn_ref, lfn_ref, kt_refs, vt_refs, lp_refs,
                      o_ref, qbd_ref, acc_ref, m_ref, l_ref, carry_ref):
    dec_seq = q_ref.shape[1]
    n_rows = dec_seq * N_HEADS

    lane_d = lax.broadcasted_iota(jnp.int32, (N_HEADS, D_MODEL), 1)
    head_d = lax.broadcasted_iota(jnp.int32, (N_HEADS, D_MODEL), 0)
    head_mask = (lane_d // HEAD_DIM == head_d).astype(F32)

    @pl.when(g == 0)
    def _():
        qf = q_ref[0].astype(F32)
        for t in range(dec_seq):
            qbd_ref[t * N_HEADS:(t + 1) * N_HEADS, :] = (head_mask * qf[t:t + 1, :]).astype(BF16)
        m_ref[...] = jnp.full(m_ref.shape, NEG, F32)
        l_ref[...] = jnp.zeros(l_ref.shape, F32)
        acc_ref[...] = jnp.zeros(acc_ref.shape, F32)
        carry_ref[...] = jnp.zeros(carry_ref.shape, F32)

    def attend(scores, lf_rows, mask, pvs):
        n = lf_rows[0].shape[1]
        local_all = _dot_exact(jnp.concatenate(lf_rows, axis=0), _upper_tri(n))
        carry = carry_ref[...]
        m_prev = m_ref[...]
        m_new = m_prev
        s = []
        for i, sc in enumerate(scores):
            local = jnp.concatenate([local_all[i * N_HEADS:(i + 1) * N_HEADS]] * dec_seq, axis=0)
            si = sc - (local + carry[:, :n])
            carry = carry + local[:, n - 1:n]
            if mask is not None:
                si = jnp.where(mask, si, NEG)
            m_new = jnp.maximum(m_new, jnp.max(si, axis=1, keepdims=True))
            s.append(si)
        alpha = jnp.exp(m_prev - m_new)
        l_new = alpha * l_ref[...]
        acc = alpha[:, :1] * acc_ref[...]
        for si, pv in zip(s, pvs):
            pr = jnp.exp(si - m_new[:, :n])
            l_new = l_new + jnp.sum(pr, axis=1, keepdims=True)
            acc = acc + pv(pr.astype(BF16))
        carry_ref[...] = carry
        m_ref[...] = m_new
        l_ref[...] = l_new
        acc_ref[...] = acc

    qbd = qbd_ref[...]
    attend([_dot(qbd, kt[...].astype(BF16)) for kt in kt_refs], [lp[...] for lp in lp_refs], None,
           [lambda pr, vt=vt: _dot_nt(pr, vt[...].astype(BF16)) for vt in vt_refs])

    @pl.when(last)
    def _():
        r = lax.broadcasted_iota(jnp.int32, (n_rows, NEW_COLS), 0)
        col = lax.broadcasted_iota(jnp.int32, (n_rows, NEW_COLS), 1)
        attend([_dot_nt(qbd_ref[...], kn_ref[0].astype(BF16))], [lfn_ref[0]], col <= r // N_HEADS,
               [lambda pr: _dot(pr, vn_ref[0].astype(BF16))])
        o = acc_ref[...] / l_ref[:, :1]
        rows = [jnp.sum(o[t * N_HEADS:(t + 1) * N_HEADS, :] * head_mask, axis=0, keepdims=True)
                for t in range(dec_seq)]
        o_ref[0] = jnp.concatenate(rows, axis=0)


def _sample_attn_conv_kernel(pt_ref, q_ref, kn_ref, vn_ref, lfn_ref, *refs, group, conv_tiles):
    kt_refs, vt_refs, lp_refs = refs[:group], refs[group:2 * group], refs[2 * group:3 * group]
    conv_in_refs = refs[3 * group:3 * group + 5]
    o_ref, z_ref, st_ref = refs[3 * group + 5:3 * group + 8]
    attn_scratch = refs[3 * group + 8:3 * group + 13]
    conv_scratch = refs[3 * group + 13:]
    g = pl.program_id(1)
    step = pl.program_id(0) * pl.num_programs(1) + g
    _sample_attn_step(g, g == pl.num_programs(1) - 1, q_ref, kn_ref, vn_ref, lfn_ref, kt_refs, vt_refs, lp_refs,
                      o_ref, *attn_scratch)
    _conv_mid_step(step % conv_tiles, *conv_in_refs, z_ref, st_ref, *conv_scratch)


def sample_attention_and_conv_mid(page_ids, q, k_new, v_new, lft_new, cache_kt, cache_vt, cache_lft,
                                  u, w_dw, b_dw, ln_g, ln_b, *, group):
    db, n_pages = page_ids.shape
    dec_seq = q.shape[1]
    n_rows = dec_seq * N_HEADS
    n_groups = n_pages // group
    bsz, seq, _ = u.shape
    tt = bsz * seq // (db * n_groups)
    conv_tiles = seq // tt
    assert tt * db * n_groups == bsz * seq and conv_tiles * tt == seq and tt % SUBLANES == 0 and tt >= HIST_PAD

    def page_spec(rows, i):
        return pl.BlockSpec((None, rows, PAGE), lambda b, g, pt: (pt[b, g * group + i], 0, 0))

    def per_batch(rows, width):
        return pl.BlockSpec((1, rows, width), lambda b, g, pt: (b, 0, 0))

    conv_tile = pl.BlockSpec((1, tt, D_MODEL), lambda b, g, pt: ((b * n_groups + g) // conv_tiles,
                                                                  (b * n_groups + g) % conv_tiles, 0))
    grid_spec = pltpu.PrefetchScalarGridSpec(
        num_scalar_prefetch=1,
        grid=(db, n_groups),
        in_specs=([per_batch(dec_seq, D_MODEL), per_batch(NEW_COLS, D_MODEL), per_batch(NEW_COLS, D_MODEL),
                   per_batch(N_HEADS, NEW_COLS)]
                  + [page_spec(D_MODEL, i) for i in range(group)]
                  + [page_spec(D_MODEL, i) for i in range(group)]
                  + [page_spec(N_HEADS, i) for i in range(group)]
                  + [conv_tile, _resident((CONV_WIDTH, D_MODEL)), _resident((1, D_MODEL)),
                     _resident((1, D_MODEL)), _resident((1, D_MODEL))]),
        out_specs=(per_batch(dec_seq, D_MODEL), conv_tile,
                   pl.BlockSpec((1, CONV_HIST, D_MODEL), lambda b, g, pt: ((b * n_groups + g) // conv_tiles, 0, 0))),
        scratch_shapes=[pltpu.VMEM((n_rows, D_MODEL), BF16),
                        pltpu.VMEM((n_rows, D_MODEL), F32),
                        pltpu.VMEM((n_rows, LANES), F32),
                        pltpu.VMEM((n_rows, LANES), F32),
                        pltpu.VMEM((n_rows, LANES), F32),
                        pltpu.VMEM((SUBLANES, HIST_PAD + tt, D_MODEL), F32),
                        pltpu.VMEM((tt, D_MODEL), F32)],
    )
    return pl.pallas_call(
        functools.partial(_sample_attn_conv_kernel, group=group, conv_tiles=conv_tiles),
        out_shape=(jax.ShapeDtypeStruct((db, dec_seq, D_MODEL), F32),
                   jax.ShapeDtypeStruct((bsz, seq, D_MODEL), BF16),
                   jax.ShapeDtypeStruct((bsz, CONV_HIST, D_MODEL), F32)),
        grid_spec=grid_spec,
        compiler_params=_params("arbitrary", "arbitrary"),
        name="sample_attention_and_conv_mid",
    )(page_ids, q, k_new, v_new, lft_new, *([cache_kt] * group), *([cache_vt] * group), *([cache_lft] * group),
      u, w_dw, b_dw, ln_g, ln_b)


PROMPT_TM = 512
FFN_CHUNK = 2816
ATTN_TQ = 512
ATTN_CB = 256
ATTN_KC = 256
ATTN_HPI = 16
PAGE_GROUP = 16


def kernel(x_prompt, x_sample, state_conv, cache_k, cache_v, cache_logf, page_table, norm_mix_g, norm_ffn_g,
           norm_final_g, conv_w_in, conv_b_in, conv_w_dw, conv_b_dw, conv_ln_g, conv_ln_b, conv_w_out, conv_b_out,
           attn_w_in, attn_b_f, attn_w_out, ffn_w_gu, ffn_w_down):
    bsz, seq, d = x_prompt.shape
    dbsz, dseq, _ = x_sample.shape
    n_sample = dbsz * dseq
    row = lambda a: a.reshape(1, -1)
    to_time_major = lambda a: jnp.transpose(a.reshape(dbsz, dseq, -1), (1, 0, 2)).reshape(n_sample, -1)
    to_batch_major = lambda a: jnp.transpose(a.reshape(dseq, dbsz, -1), (1, 0, 2))

    w_in = conv_w_in[0].astype(BF16)
    w_out = conv_w_out[0].astype(BF16)
    wgu = ffn_w_gu.astype(BF16)
    wd = ffn_w_down.astype(BF16)
    conv_args = (conv_w_dw[0], row(conv_b_dw[0]), row(conv_ln_g[0]), row(conv_ln_b[0]))
    w_in_t = jnp.transpose(attn_w_in[0]).astype(BF16)
    wq = attn_w_in[0, :, :d].astype(BF16)
    wft = w_in_t[3 * d:]
    bf_col = attn_b_f[0].reshape(N_HEADS, 1)
    wo1 = attn_w_out[0].astype(BF16)

    hp = x_prompt.reshape(bsz * seq, d)
    hs = to_time_major(x_sample)

    us = conv_in(hs, row(norm_mix_g[0]), w_in, row(conv_b_in[0]), tm=n_sample)
    zs, conv_state_s = conv_mid_sample(us.reshape(dseq, dbsz, d), jnp.transpose(state_conv[0], (1, 0, 2)),
                                       *conv_args)
    hs = mix_ffn(hs, zs.reshape(n_sample, d), w_out, row(conv_b_out[0]), row(norm_ffn_g[0]), wgu, wd, None,
                 layer=0, tm=n_sample, fc=FFN_CHUNK)
    qs, ks, vs, lfts = attn_proj(hs.reshape(1, n_sample, d), row(norm_mix_g[1]), wq, w_in_t[d:3 * d], wft, bf_col,
                                 tm=n_sample)
    ks_b = to_batch_major(ks[0])
    vs_b = to_batch_major(vs[0])
    lfts_b = jnp.transpose(lfts.reshape(N_HEADS, dseq, dbsz), (2, 0, 1))
    pad_rows = lambda a: jnp.pad(a, ((0, 0), (0, NEW_COLS - dseq), (0, 0)))
    cache_kt = jnp.transpose(cache_k, (0, 1, 3, 4, 2)).reshape(-1, d, PAGE)
    cache_vt = jnp.transpose(cache_v, (0, 1, 3, 4, 2)).reshape(-1, d, PAGE)
    cache_lft = jnp.transpose(cache_logf, (0, 1, 3, 2)).reshape(-1, N_HEADS, PAGE)

    up = conv_in(hp, row(norm_mix_g[0]), w_in, row(conv_b_in[0]), tm=PROMPT_TM)
    os_, zp, conv_state_p = sample_attention_and_conv_mid(
        page_table, to_batch_major(qs[0]), pad_rows(ks_b), pad_rows(vs_b),
        jnp.pad(lfts_b, ((0, 0), (0, 0), (0, NEW_COLS - dseq))), cache_kt, cache_vt, cache_lft,
        up.reshape(bsz, seq, d), *conv_args, group=PAGE_GROUP)
    hp = mix_ffn(hp, zp.reshape(bsz * seq, d), w_out, row(conv_b_out[0]), row(norm_ffn_g[0]), wgu, wd, None,
                 layer=0, tm=PROMPT_TM, fc=FFN_CHUNK)

    ktp, vtp, lftp, q_slabs, k_slabs, v_slabs = attn_proj_prompt(
        hp.reshape(bsz, seq, d), row(norm_mix_g[1]), w_in_t[:3 * d], wft, bf_col, tm=PROMPT_TM)
    k_slabs = cumsum_into_slabs(lftp, k_slabs)
    op = flash_prompt(q_slabs, k_slabs, v_slabs, tq=ATTN_TQ, cb=ATTN_CB, kc=ATTN_KC, hpi=ATTN_HPI)
    hp = mix_ffn(hp, op.reshape(bsz * seq, d), wo1, None, row(norm_ffn_g[1]), wgu, wd, row(norm_final_g),
                 layer=1, tm=PROMPT_TM, fc=FFN_CHUNK)
    hs = mix_ffn(hs, to_time_major(os_), wo1, None, row(norm_ffn_g[1]), wgu, wd, row(norm_final_g),
                 layer=1, tm=n_sample, fc=FFN_CHUNK)

    heads_t = lambda a: jnp.transpose(a.reshape(bsz, N_HEADS, HEAD_DIM, seq), (0, 3, 1, 2))[None]
    heads = lambda a: a.reshape(1, dbsz, dseq, N_HEADS, HEAD_DIM)
    return (hp.reshape(bsz, seq, d), to_batch_major(hs),
            conv_state_p[None], jnp.transpose(conv_state_s, (1, 0, 2))[None],
            heads_t(ktp), heads_t(vtp), jnp.transpose(lftp, (0, 2, 1))[None],
            heads(ks_b), heads(vs_b), jnp.transpose(lfts_b, (0, 2, 1))[None])
```

```python
import functools

import jax
import jax.numpy as jnp
from jax import lax
from jax.experimental import pallas as pl
from jax.experimental.pallas import tpu as pltpu

D_MODEL = 1024
N_HEADS = 16
HEAD_DIM = 64
N_PAIRS = N_HEADS // 2
LANES = 128
D_FF = 2816
CONV_WIDTH = 31
CONV_HIST = CONV_WIDTH - 1
HIST_PAD = 32
EPS = 1e-6
PAGE = 128
NEG = -1e30
VMEM_LIMIT_BYTES = 56 * 1024 * 1024

F32 = jnp.float32
BF16 = jnp.bfloat16
NT_DIMS = (((1,), (1,)), ((), ()))
TN_DIMS = (((0,), (0,)), ((), ()))


def _params(*sem):
    return pltpu.CompilerParams(dimension_semantics=sem, vmem_limit_bytes=VMEM_LIMIT_BYTES)


def _resident(shape, layer=None):
    nd = len(shape)
    if layer is None:
        return pl.BlockSpec(shape, lambda *_: (0,) * nd, pipeline_mode=pl.Buffered(1))
    return pl.BlockSpec((None,) + shape, lambda *_: (layer,) + (0,) * nd, pipeline_mode=pl.Buffered(1))


def _rmsnorm(x, g):
    return x * lax.rsqrt(jnp.mean(x * x, axis=-1, keepdims=True) + EPS) * g


def _sigmoid(x):
    return 1.0 / (1.0 + jnp.exp(-x))


def _log_sigmoid(x):
    return jnp.minimum(x, 0.0) - jnp.log1p(jnp.exp(-jnp.abs(x)))


def _layernorm_silu(y, g, b):
    mu = jnp.mean(y, axis=-1, keepdims=True)
    yc = y - mu
    var = jnp.mean(yc * yc, axis=-1, keepdims=True)
    yn = yc * lax.rsqrt(var + EPS) * g + b
    return yn * _sigmoid(yn)


def _dot(a, b):
    return jnp.dot(a, b, preferred_element_type=F32)


def _dot_nt(a, b):
    return lax.dot_general(a, b, NT_DIMS, preferred_element_type=F32)


def _cumsum_lanes(x):
    n = x.shape[1]
    tri = (lax.broadcasted_iota(jnp.int32, (n, n), 0) <= lax.broadcasted_iota(jnp.int32, (n, n), 1)).astype(BF16)
    hi = x.astype(BF16)
    rest = x - hi.astype(F32)
    mid = rest.astype(BF16)
    lo = (rest - mid.astype(F32)).astype(BF16)
    return _dot(hi, tri) + _dot(mid, tri) + _dot(lo, tri)


def _conv_in_kernel(h_ref, g_ref, w_ref, b_ref, u_ref):
    xn = _rmsnorm(h_ref[...], g_ref[...]).astype(BF16)
    hh = _dot(xn, w_ref[...]) + b_ref[...]
    u_ref[...] = hh[:, :D_MODEL] * _sigmoid(hh[:, D_MODEL:])


def conv_in(h, g, w_bf, b, *, tm):
    m = h.shape[0]
    return pl.pallas_call(
        _conv_in_kernel,
        out_shape=jax.ShapeDtypeStruct((m, D_MODEL), F32),
        grid=(m // tm,),
        in_specs=[pl.BlockSpec((tm, D_MODEL), lambda i: (i, 0)),
                  _resident((1, D_MODEL)),
                  _resident((D_MODEL, 2 * D_MODEL)),
                  _resident((1, 2 * D_MODEL))],
        out_specs=pl.BlockSpec((tm, D_MODEL), lambda i: (i, 0)),
        compiler_params=_params("parallel"),
        name="conv_in",
    )(h, g, w_bf, b)


SUBLANES = 8
CONV_RC = 16


def _conv_mid_step(t, u_ref, wdw_ref, bdw_ref, lng_ref, lnb_ref, z_ref, st_ref, ubuf, ybuf):
    tt = u_ref.shape[1]
    n_shift = ubuf.shape[1]

    @pl.when(t == 0)
    def _():
        ubuf[0, 0:HIST_PAD, :] = jnp.zeros((HIST_PAD, D_MODEL), F32)

    @pl.when(t > 0)
    def _():
        ubuf[0, 0:HIST_PAD, :] = ubuf[0, tt:tt + HIST_PAD, :]

    ubuf[0, HIST_PAD:HIST_PAD + tt, :] = u_ref[0]
    st_ref[0] = ubuf[0, tt + HIST_PAD - CONV_HIST:tt + HIST_PAD, :]
    for s in range(1, SUBLANES):
        ubuf[s, 0:n_shift - SUBLANES, :] = ubuf[0, s:s + n_shift - SUBLANES, :]

    off = HIST_PAD - CONV_HIST
    rc = min(tt, CONV_RC)

    def row_chunk(i, carry):
        r0 = pl.multiple_of(i * rc, rc)
        for l0 in range(0, D_MODEL, LANES):
            acc = jnp.broadcast_to(bdw_ref[:, l0:l0 + LANES], (rc, LANES))
            for j in range(CONV_WIDTH):
                shift, base = (j + off) % SUBLANES, (j + off) // SUBLANES * SUBLANES
                acc = acc + wdw_ref[j:j + 1, l0:l0 + LANES] * ubuf[shift, pl.ds(r0 + base, rc), l0:l0 + LANES]
            ybuf[pl.ds(r0, rc), l0:l0 + LANES] = acc
        return carry

    lax.fori_loop(0, tt // rc, row_chunk, 0)

    z_ref[0] = _layernorm_silu(ybuf[...], lng_ref[...], lnb_ref[...]).astype(z_ref.dtype)


def _conv_mid_sample_kernel(u_ref, hist_ref, wdw_ref, bdw_ref, lng_ref, lnb_ref, z_ref, st_ref, upad):
    dseq = u_ref.shape[0]
    upad[0:CONV_HIST] = hist_ref[...]
    upad[CONV_HIST:CONV_HIST + dseq] = u_ref[...]
    st_ref[...] = upad[dseq:dseq + CONV_HIST]
    for t in range(dseq):
        acc = jnp.broadcast_to(bdw_ref[...], upad.shape[1:])
        for j in range(CONV_WIDTH):
            acc = acc + wdw_ref[j:j + 1, :] * upad[t + j]
        z_ref[t] = _layernorm_silu(acc, lng_ref[...], lnb_ref[...])


def conv_mid_sample(u, hist, w_dw, b_dw, ln_g, ln_b):
    dseq, db, _ = u.shape
    whole = lambda shape: pl.BlockSpec(shape, lambda i: (0,) * len(shape))
    return pl.pallas_call(
        _conv_mid_sample_kernel,
        out_shape=(jax.ShapeDtypeStruct((dseq, db, D_MODEL), F32),
                   jax.ShapeDtypeStruct((CONV_HIST, db, D_MODEL), F32)),
        grid=(1,),
        in_specs=[whole(u.shape), whole(hist.shape), whole(w_dw.shape), whole(b_dw.shape), whole(ln_g.shape),
                  whole(ln_b.shape)],
        out_specs=(whole((dseq, db, D_MODEL)), whole((CONV_HIST, db, D_MODEL))),
        scratch_shapes=[pltpu.VMEM((CONV_HIST + dseq, db, D_MODEL), F32)],
        compiler_params=_params("arbitrary"),
        name="conv_mid_sample",
    )(u, hist, w_dw, b_dw, ln_g, ln_b)


def _mix_ffn_kernel(*refs, fc, has_bias, final):
    h_ref, z_ref, wo_ref = refs[:3]
    refs = refs[3:]
    if has_bias:
        bo_ref, refs = refs[0], refs[1:]
    g_ref, wgu_ref, wd_ref = refs[:3]
    refs = refs[3:]
    if final:
        gf_ref, refs = refs[0], refs[1:]
    out_ref = refs[0]

    h1 = h_ref[...] + _dot(z_ref[...].astype(BF16), wo_ref[...])
    if has_bias:
        h1 = h1 + bo_ref[...]
    xn = _rmsnorm(h1, g_ref[...]).astype(BF16)
    acc = h1
    for c in range(D_FF // fc):
        gate = _dot(xn, wgu_ref[:, c * fc:(c + 1) * fc])
        up = _dot(xn, wgu_ref[:, D_FF + c * fc:D_FF + (c + 1) * fc])
        a = (gate * _sigmoid(gate) * up).astype(BF16)
        acc = acc + _dot(a, wd_ref[c * fc:(c + 1) * fc, :])
    out_ref[...] = _rmsnorm(acc, gf_ref[...]) if final else acc


def mix_ffn(h, z, wo_bf, bo, g_ffn, wgu_bf, wd_bf, g_final, *, layer, tm, fc):
    m = h.shape[0]
    has_bias = bo is not None
    final = g_final is not None
    row = pl.BlockSpec((tm, D_MODEL), lambda i: (i, 0))
    args = [h, z, wo_bf]
    specs = [row, row, _resident((D_MODEL, D_MODEL))]
    if has_bias:
        args.append(bo)
        specs.append(_resident((1, D_MODEL)))
    args += [g_ffn, wgu_bf, wd_bf]
    specs += [_resident((1, D_MODEL)), _resident((D_MODEL, 2 * D_FF), layer), _resident((D_FF, D_MODEL), layer)]
    if final:
        args.append(g_final)
        specs.append(_resident((1, D_MODEL)))
    return pl.pallas_call(
        functools.partial(_mix_ffn_kernel, fc=fc, has_bias=has_bias, final=final),
        out_shape=jax.ShapeDtypeStruct((m, D_MODEL), F32),
        grid=(m // tm,),
        in_specs=specs,
        out_specs=row,
        compiler_params=_params("parallel"),
        name="mix_ffn",
    )(*args)


C_ROW = HEAD_DIM
C_TERMS = 3
C_ROWS = 16
L_ROW = HEAD_DIM
V_ROWS = HEAD_DIM + 16
LOG2E = 1.4426950408889634


def _attn_proj_prompt_kernel(h_ref, g_ref, wt_ref, wft_ref, bf_ref, k_ref, v_ref, lf_ref, qs_ref, ks_ref, vs_ref):
    tm = h_ref.shape[1]
    xn = _rmsnorm(h_ref[0], g_ref[...]).astype(BF16)
    qkvt = _dot_nt(wt_ref[...], xn)
    k_ref[0] = qkvt[D_MODEL:2 * D_MODEL]
    v_ref[0] = qkvt[2 * D_MODEL:]
    lf_ref[0] = _log_sigmoid(_dot_nt(wft_ref[...], xn) + bf_ref[...])

    row = lax.broadcasted_iota(jnp.int32, (HEAD_DIM, 1), 0)
    q_tail = jnp.broadcast_to(jnp.where(row < C_TERMS, -1.0, 0.0), (HEAD_DIM, tm)).astype(BF16)
    k_tail = jnp.zeros((HEAD_DIM, tm), BF16)
    v_tail = jnp.broadcast_to(jnp.where(row[:V_ROWS - HEAD_DIM] == 0, 1.0, 0.0), (V_ROWS - HEAD_DIM, tm)).astype(BF16)
    for h in range(N_HEADS):
        head = slice(h * HEAD_DIM, (h + 1) * HEAD_DIM)
        qs_ref[0, h, :HEAD_DIM] = (qkvt[:D_MODEL][head] * (HEAD_DIM ** -0.5 * LOG2E)).astype(BF16)
        qs_ref[0, h, HEAD_DIM:] = q_tail
        ks_ref[0, h, :HEAD_DIM] = qkvt[D_MODEL:2 * D_MODEL][head].astype(BF16)
        ks_ref[0, h, HEAD_DIM:] = k_tail
        vs_ref[0, h, :HEAD_DIM] = qkvt[2 * D_MODEL:][head].astype(BF16)
        vs_ref[0, h, HEAD_DIM:] = v_tail


def attn_proj_prompt(h, g, wt_bf, wft_bf, bf_col, *, tm):
    b, s, _ = h.shape
    col = pl.BlockSpec((1, D_MODEL, tm), lambda i, j: (i, 0, j))
    slab = pl.BlockSpec((1, N_HEADS, LANES, tm), lambda i, j: (i, 0, 0, j))
    return pl.pallas_call(
        _attn_proj_prompt_kernel,
        out_shape=(jax.ShapeDtypeStruct((b, D_MODEL, s), F32), jax.ShapeDtypeStruct((b, D_MODEL, s), F32),
                   jax.ShapeDtypeStruct((b, N_HEADS, s), F32),
                   jax.ShapeDtypeStruct((b, N_HEADS, LANES, s), BF16),
                   jax.ShapeDtypeStruct((b, N_HEADS, LANES, s), BF16),
                   jax.ShapeDtypeStruct((b, N_HEADS, V_ROWS, s), BF16)),
        grid=(b, s // tm),
        in_specs=[pl.BlockSpec((1, tm, D_MODEL), lambda i, j: (i, j, 0)),
                  _resident((1, D_MODEL)),
                  _resident((3 * D_MODEL, D_MODEL)),
                  _resident((N_HEADS, D_MODEL)),
                  _resident((N_HEADS, 1))],
        out_specs=(col, col,
                   pl.BlockSpec((1, N_HEADS, tm), lambda i, j: (i, 0, j)),
                   slab, slab,
                   pl.BlockSpec((1, N_HEADS, V_ROWS, tm), lambda i, j: (i, 0, 0, j))),
        compiler_params=_params("parallel", "parallel"),
        name="attn_proj_prompt",
    )(h, g, wt_bf, wft_bf, bf_col)


def _attn_proj_kernel(h_ref, g_ref, wq_ref, wkvt_ref, wft_ref, bf_ref, q_ref, k_ref, v_ref, lf_ref):
    xn = _rmsnorm(h_ref[0], g_ref[...]).astype(BF16)
    q_ref[0] = (_dot(xn, wq_ref[...]) * (HEAD_DIM ** -0.5)).astype(BF16)
    kv = _dot_nt(wkvt_ref[...], xn).T
    k_ref[0] = kv[:, :D_MODEL]
    v_ref[0] = kv[:, D_MODEL:]
    lf_ref[0] = _log_sigmoid(_dot_nt(wft_ref[...], xn) + bf_ref[...])


def attn_proj(h, g, wq_bf, wkvt_bf, wft_bf, bf_col, *, tm):
    b, s, _ = h.shape
    row = pl.BlockSpec((1, tm, D_MODEL), lambda i, j: (i, j, 0))
    out_shape = [jax.ShapeDtypeStruct((b, s, D_MODEL), BF16),
                 jax.ShapeDtypeStruct((b, s, D_MODEL), F32), jax.ShapeDtypeStruct((b, s, D_MODEL), F32),
                 jax.ShapeDtypeStruct((b, N_HEADS, s), F32)]
    out_specs = [row, row, row, pl.BlockSpec((1, N_HEADS, tm), lambda i, j: (i, 0, j))]
    return pl.pallas_call(
        _attn_proj_kernel,
        out_shape=tuple(out_shape),
        grid=(b, s // tm),
        in_specs=[row,
                  _resident((1, D_MODEL)),
                  _resident((D_MODEL, D_MODEL)),
                  _resident((2 * D_MODEL, D_MODEL)),
                  _resident((N_HEADS, D_MODEL)),
                  _resident((N_HEADS, 1))],
        out_specs=tuple(out_specs),
        compiler_params=_params("parallel", "parallel"),
        name="attn_proj",
    )(h, g, wq_bf, wkvt_bf, wft_bf, bf_col)


CUMSUM_CHUNK = 256


def _cumsum_kernel(x_ref, slab_hbm_ref, c_ref):
    del slab_hbm_ref
    s = x_ref.shape[2]
    row = lax.broadcasted_iota(jnp.int32, (C_ROWS, CUMSUM_CHUNK), 0)
    carry = jnp.zeros((N_HEADS, 1), F32)
    for j in range(s // CUMSUM_CHUNK):
        sl = slice(j * CUMSUM_CHUNK, (j + 1) * CUMSUM_CHUNK)
        local = _cumsum_lanes(x_ref[0, :, sl])
        c = (local + carry) * LOG2E
        carry = carry + local[:, CUMSUM_CHUNK - 1:CUMSUM_CHUNK]
        hi = c.astype(BF16).astype(F32)
        mid = (c - hi).astype(BF16).astype(F32)
        lo = c - hi - mid
        for h in range(N_HEADS):
            rows = jnp.where(row == 0, hi[h:h + 1], jnp.where(row == 1, mid[h:h + 1],
                                                               jnp.where(row == 2, lo[h:h + 1], 0.0)))
            c_ref[0, h, :, sl] = rows.astype(BF16)


def cumsum_into_slabs(lft, k_slabs):
    b, h, s = lft.shape
    return pl.pallas_call(
        _cumsum_kernel,
        out_shape=jax.ShapeDtypeStruct(k_slabs.shape, k_slabs.dtype),
        grid=(b,),
        in_specs=[pl.BlockSpec((1, h, s), lambda i: (i, 0, 0)),
                  pl.BlockSpec(memory_space=pl.ANY)],
        out_specs=pl.BlockSpec((1, h, C_ROWS, s), lambda i: (i, 0, C_ROW // C_ROWS, 0)),
        input_output_aliases={1: 0},
        compiler_params=_params("parallel"),
        name="cumsum_into_slabs",
    )(lft, k_slabs)


def _flash_kernel(qi_ref, ki_ref, q_ref, k_ref, v_ref, o_ref, acc_ref, m_ref, *, tq, cb, kc, hpi):
    t = pl.program_id(1)
    qi = qi_ref[t]
    ki = ki_ref[t]

    @pl.when(ki == 0)
    def _():
        m_ref[...] = jnp.full(m_ref.shape, NEG, F32)
        acc_ref[...] = jnp.zeros(acc_ref.shape, F32)

    def sweep(diagonal):
        def heads(j, carry):
            chains = [(j * hpi + hh, c) for hh in range(hpi) for c in range(tq // cb)]
            cols = [slice(c * cb, (c + 1) * cb) for _, c in chains]
            nk = [(c + 1) * cb if diagonal else tq for _, c in chains]
            n = range(len(chains))
            for k0 in range(0, tq, kc):
                live = [i for i in n if k0 < nk[i]]
                s = {i: lax.dot_general(k_ref[0, chains[i][0], :, k0:k0 + kc], q_ref[0, chains[i][0], :, cols[i]],
                                        TN_DIMS, preferred_element_type=F32)
                     for i in live}
                for i in live:
                    si = s[i]
                    if diagonal and k0 + kc > chains[i][1] * cb:
                        kpos = lax.broadcasted_iota(jnp.int32, si.shape, 0) + k0
                        qpos = lax.broadcasted_iota(jnp.int32, si.shape, 1) + chains[i][1] * cb
                        si = jnp.where(kpos <= qpos, si, NEG)
                    h = chains[i][0]
                    m_prev = m_ref[h, :, cols[i]]
                    m_new = jnp.maximum(m_prev, jnp.max(si, axis=0, keepdims=True))
                    pr = jnp.exp2(si - m_new[:1])
                    acc_ref[h, :, cols[i]] = (jnp.exp2(m_prev - m_new)[:1] * acc_ref[h, :, cols[i]]
                                              + _dot(v_ref[0, h, :, k0:k0 + kc], pr.astype(BF16)))
                    m_ref[h, :, cols[i]] = m_new
            return carry

        lax.fori_loop(0, N_HEADS // hpi, heads, 0)

    @pl.when(ki < qi)
    def _():
        sweep(False)

    @pl.when(ki == qi)
    def _():
        sweep(True)
        for p in range(N_PAIRS):
            pair = [acc_ref[2 * p + hh] for hh in range(2)]
            pair = [a[:HEAD_DIM] / a[L_ROW:L_ROW + 1] for a in pair]
            o_ref[0, :, p * LANES:(p + 1) * LANES] = jnp.concatenate(pair, axis=0).T.astype(o_ref.dtype)


def flash_prompt(q_slabs, k_slabs, v_slabs, *, tq, cb, kc, hpi):
    b, _, _, s = q_slabs.shape
    nq = s // tq
    qi = jnp.asarray([i for i in range(nq) for _ in range(i + 1)], jnp.int32)
    ki = jnp.asarray([j for i in range(nq) for j in range(i + 1)], jnp.int32)
    grid_spec = pltpu.PrefetchScalarGridSpec(
        num_scalar_prefetch=2,
        grid=(b, int(qi.shape[0])),
        in_specs=[pl.BlockSpec((1, N_HEADS, LANES, tq), lambda i, t, qi, ki: (i, 0, 0, qi[t])),
                  pl.BlockSpec((1, N_HEADS, LANES, tq), lambda i, t, qi, ki: (i, 0, 0, ki[t])),
                  pl.BlockSpec((1, N_HEADS, V_ROWS, tq), lambda i, t, qi, ki: (i, 0, 0, ki[t]))],
        out_specs=pl.BlockSpec((1, tq, D_MODEL), lambda i, t, qi, ki: (i, qi[t], 0)),
        scratch_shapes=[pltpu.VMEM((N_HEADS, V_ROWS, tq), F32),
                        pltpu.VMEM((N_HEADS, SUBLANES, tq), F32)],
    )
    return pl.pallas_call(
        functools.partial(_flash_kernel, tq=tq, cb=cb, kc=kc, hpi=hpi),
        out_shape=jax.ShapeDtypeStruct((b, s, D_MODEL), BF16),
        grid_spec=grid_spec,
        compiler_params=_params("parallel", "arbitrary"),
        name="flash_prompt",
    )(qi, ki, q_slabs, k_slabs, v_slabs)


NEW_COLS = 16
PAGES_PER_DOT = 4


def _sample_attn_step(g, last, q_ref, kn_ref, vn_ref, lfn_ref, kt_refs, vt_refs, lp_refs,
                      o_ref, qbd_ref, acc_ref, m_ref, l_ref, carry_ref):
    dec_seq = q_ref.shape[1]
    n_rows = dec_seq * N_HEADS

    lane_d = lax.broadcasted_iota(jnp.int32, (N_HEADS, D_MODEL), 1)
    head_d = lax.broadcasted_iota(jnp.int32, (N_HEADS, D_MODEL), 0)
    head_mask = (lane_d // HEAD_DIM == head_d).astype(F32)

    @pl.when(g == 0)
    def _():
        qf = q_ref[0].astype(F32)
        for t in range(dec_seq):
            qbd_ref[t * N_HEADS:(t + 1) * N_HEADS, :] = (head_mask * qf[t:t + 1, :]).astype(BF16)
        m_ref[...] = jnp.full(m_ref.shape, NEG, F32)
        l_ref[...] = jnp.zeros(l_ref.shape, F32)
        acc_ref[...] = jnp.zeros(acc_ref.shape, F32)
        carry_ref[...] = jnp.zeros(carry_ref.shape, F32)

    def attend(scores, lf_rows, mask, pvs):
        n = lf_rows[0].shape[1]
        per = len(lf_rows) // len(scores)
        local_all = _cumsum_lanes(jnp.concatenate(lf_rows, axis=0))
        carry = carry_ref[...]
        m_prev = m_ref[...]
        m_new = m_prev
        s = []
        for i, sc in enumerate(scores):
            cs = []
            for r in range(i * per, (i + 1) * per):
                local = jnp.concatenate([local_all[r * N_HEADS:(r + 1) * N_HEADS]] * dec_seq, axis=0)
                cs.append(local + carry[:, :n])
                carry = carry + local[:, n - 1:n]
            si = sc - jnp.concatenate(cs, axis=1)
            if mask is not None:
                si = jnp.where(mask, si, NEG)
            m_new = jnp.maximum(m_new, jnp.max(si, axis=1, keepdims=True))
            s.append(si)
        alpha = jnp.exp(m_prev - m_new)
        l_new = alpha * l_ref[...]
        acc = alpha[:, :1] * acc_ref[...]
        for si, pv in zip(s, pvs):
            pr = jnp.exp(si - jnp.concatenate([m_new[:, :n]] * per, axis=1))
            l_new = l_new + jnp.sum(pr, axis=1, keepdims=True)
            acc = acc + pv(pr.astype(BF16))
        carry_ref[...] = carry
        m_ref[...] = m_new
        l_ref[...] = l_new
        acc_ref[...] = acc

    def pages(refs, i):
        return jnp.concatenate([r[...].astype(BF16) for r in refs[i:i + PAGES_PER_DOT]], axis=1)

    qbd = qbd_ref[...]
    firsts = range(0, len(kt_refs), PAGES_PER_DOT)
    attend([_dot(qbd, pages(kt_refs, i)) for i in firsts], [lp[...] for lp in lp_refs], None,
           [lambda pr, i=i: _dot_nt(pr, pages(vt_refs, i)) for i in firsts])

    @pl.when(last)
    def _():
        r = lax.broadcasted_iota(jnp.int32, (n_rows, NEW_COLS), 0)
        col = lax.broadcasted_iota(jnp.int32, (n_rows, NEW_COLS), 1)
        attend([_dot_nt(qbd_ref[...], kn_ref[0].astype(BF16))], [lfn_ref[0]], col <= r // N_HEADS,
               [lambda pr: _dot(pr, vn_ref[0].astype(BF16))])
        o = acc_ref[...] / l_ref[:, :1]
        rows = [jnp.sum(o[t * N_HEADS:(t + 1) * N_HEADS, :] * head_mask, axis=0, keepdims=True)
                for t in range(dec_seq)]
        o_ref[0] = jnp.concatenate(rows, axis=0)


def _sample_attn_conv_kernel(pt_ref, q_ref, kn_ref, vn_ref, lfn_ref, *refs, group, conv_tiles):
    kt_refs, vt_refs, lp_refs = refs[:group], refs[group:2 * group], refs[2 * group:3 * group]
    conv_in_refs = refs[3 * group:3 * group + 5]
    o_ref, z_ref, st_ref = refs[3 * group + 5:3 * group + 8]
    attn_scratch = refs[3 * group + 8:3 * group + 13]
    conv_scratch = refs[3 * group + 13:]
    g = pl.program_id(1)
    step = pl.program_id(0) * pl.num_programs(1) + g
    _sample_attn_step(g, g == pl.num_programs(1) - 1, q_ref, kn_ref, vn_ref, lfn_ref, kt_refs, vt_refs, lp_refs,
                      o_ref, *attn_scratch)
    _conv_mid_step(step % conv_tiles, *conv_in_refs, z_ref, st_ref, *conv_scratch)


def sample_attention_and_conv_mid(page_ids, q, k_new, v_new, lft_new, cache_kt, cache_vt, cache_lft,
                                  u, w_dw, b_dw, ln_g, ln_b, *, group):
    db, n_pages = page_ids.shape
    dec_seq = q.shape[1]
    n_rows = dec_seq * N_HEADS
    n_groups = n_pages // group
    bsz, seq, _ = u.shape
    tt = bsz * seq // (db * n_groups)
    conv_tiles = seq // tt
    assert tt * db * n_groups == bsz * seq and conv_tiles * tt == seq and tt % SUBLANES == 0 and tt >= HIST_PAD

    def page_spec(rows, i):
        return pl.BlockSpec((None, rows, PAGE), lambda b, g, pt: (pt[b, g * group + i], 0, 0))

    def per_batch(rows, width):
        return pl.BlockSpec((1, rows, width), lambda b, g, pt: (b, 0, 0))

    conv_tile = pl.BlockSpec((1, tt, D_MODEL), lambda b, g, pt: ((b * n_groups + g) // conv_tiles,
                                                                  (b * n_groups + g) % conv_tiles, 0))
    grid_spec = pltpu.PrefetchScalarGridSpec(
        num_scalar_prefetch=1,
        grid=(db, n_groups),
        in_specs=([per_batch(dec_seq, D_MODEL), per_batch(NEW_COLS, D_MODEL), per_batch(NEW_COLS, D_MODEL),
                   per_batch(N_HEADS, NEW_COLS)]
                  + [page_spec(D_MODEL, i) for i in range(group)]
                  + [page_spec(D_MODEL, i) for i in range(group)]
                  + [page_spec(N_HEADS, i) for i in range(group)]
                  + [conv_tile, _resident((CONV_WIDTH, D_MODEL)), _resident((1, D_MODEL)),
                     _resident((1, D_MODEL)), _resident((1, D_MODEL))]),
        out_specs=(per_batch(dec_seq, D_MODEL), conv_tile,
                   pl.BlockSpec((1, CONV_HIST, D_MODEL), lambda b, g, pt: ((b * n_groups + g) // conv_tiles, 0, 0))),
        scratch_shapes=[pltpu.VMEM((n_rows, D_MODEL), BF16),
                        pltpu.VMEM((n_rows, D_MODEL), F32),
                        pltpu.VMEM((n_rows, LANES), F32),
                        pltpu.VMEM((n_rows, LANES), F32),
                        pltpu.VMEM((n_rows, LANES), F32),
                        pltpu.VMEM((SUBLANES, HIST_PAD + tt, D_MODEL), F32),
                        pltpu.VMEM((tt, D_MODEL), F32)],
    )
    return pl.pallas_call(
        functools.partial(_sample_attn_conv_kernel, group=group, conv_tiles=conv_tiles),
        out_shape=(jax.ShapeDtypeStruct((db, dec_seq, D_MODEL), F32),
                   jax.ShapeDtypeStruct((bsz, seq, D_MODEL), BF16),
                   jax.ShapeDtypeStruct((bsz, CONV_HIST, D_MODEL), F32)),
        grid_spec=grid_spec,
        compiler_params=_params("arbitrary", "arbitrary"),
        name="sample_attention_and_conv_mid",
    )(page_ids, q, k_new, v_new, lft_new, *([cache_kt] * group), *([cache_vt] * group), *([cache_lft] * group),
      u, w_dw, b_dw, ln_g, ln_b)


PROMPT_TM = 512
FFN_CHUNK = 2816
ATTN_TQ = 512
ATTN_CB = 256
ATTN_KC = 256
ATTN_HPI = 16
PAGE_GROUP = 16


def kernel(x_prompt, x_sample, state_conv, cache_k, cache_v, cache_logf, page_table, norm_mix_g, norm_ffn_g,
           norm_final_g, conv_w_in, conv_b_in, conv_w_dw, conv_b_dw, conv_ln_g, conv_ln_b, conv_w_out, conv_b_out,
           attn_w_in, attn_b_f, attn_w_out, ffn_w_gu, ffn_w_down):
    bsz, seq, d = x_prompt.shape
    dbsz, dseq, _ = x_sample.shape
    n_sample = dbsz * dseq
    row = lambda a: a.reshape(1, -1)
    to_time_major = lambda a: jnp.transpose(a.reshape(dbsz, dseq, -1), (1, 0, 2)).reshape(n_sample, -1)
    to_batch_major = lambda a: jnp.transpose(a.reshape(dseq, dbsz, -1), (1, 0, 2))

    w_in = conv_w_in[0].astype(BF16)
    w_out = conv_w_out[0].astype(BF16)
    wgu = ffn_w_gu.astype(BF16)
    wd = ffn_w_down.astype(BF16)
    conv_args = (conv_w_dw[0], row(conv_b_dw[0]), row(conv_ln_g[0]), row(conv_ln_b[0]))
    w_in_t = jnp.transpose(attn_w_in[0]).astype(BF16)
    wq = attn_w_in[0, :, :d].astype(BF16)
    wft = w_in_t[3 * d:]
    bf_col = attn_b_f[0].reshape(N_HEADS, 1)
    wo1 = attn_w_out[0].astype(BF16)

    hp = x_prompt.reshape(bsz * seq, d)
    hs = to_time_major(x_sample)

    us = conv_in(hs, row(norm_mix_g[0]), w_in, row(conv_b_in[0]), tm=n_sample)
    zs, conv_state_s = conv_mid_sample(us.reshape(dseq, dbsz, d), jnp.transpose(state_conv[0], (1, 0, 2)),
                                       *conv_args)
    hs = mix_ffn(hs, zs.reshape(n_sample, d), w_out, row(conv_b_out[0]), row(norm_ffn_g[0]), wgu, wd, None,
                 layer=0, tm=n_sample, fc=FFN_CHUNK)
    qs, ks, vs, lfts = attn_proj(hs.reshape(1, n_sample, d), row(norm_mix_g[1]), wq, w_in_t[d:3 * d], wft, bf_col,
                                 tm=n_sample)
    ks_b = to_batch_major(ks[0])
    vs_b = to_batch_major(vs[0])
    lfts_b = jnp.transpose(lfts.reshape(N_HEADS, dseq, dbsz), (2, 0, 1))
    pad_rows = lambda a: jnp.pad(a, ((0, 0), (0, NEW_COLS - dseq), (0, 0)))
    cache_kt = jnp.transpose(cache_k, (0, 1, 3, 4, 2)).reshape(-1, d, PAGE)
    cache_vt = jnp.transpose(cache_v, (0, 1, 3, 4, 2)).reshape(-1, d, PAGE)
    cache_lft = jnp.transpose(cache_logf, (0, 1, 3, 2)).reshape(-1, N_HEADS, PAGE)

    up = conv_in(hp, row(norm_mix_g[0]), w_in, row(conv_b_in[0]), tm=PROMPT_TM)
    os_, zp, conv_state_p = sample_attention_and_conv_mid(
        page_table, to_batch_major(qs[0]), pad_rows(ks_b), pad_rows(vs_b),
        jnp.pad(lfts_b, ((0, 0), (0, 0), (0, NEW_COLS - dseq))), cache_kt, cache_vt, cache_lft,
        up.reshape(bsz, seq, d), *conv_args, group=PAGE_GROUP)
    hp = mix_ffn(hp, zp.reshape(bsz * seq, d), w_out, row(conv_b_out[0]), row(norm_ffn_g[0]), wgu, wd, None,
                 layer=0, tm=PROMPT_TM, fc=FFN_CHUNK)

    ktp, vtp, lftp, q_slabs, k_slabs, v_slabs = attn_proj_prompt(
        hp.reshape(bsz, seq, d), row(norm_mix_g[1]), w_in_t[:3 * d], wft, bf_col, tm=PROMPT_TM)
    k_slabs = cumsum_into_slabs(lftp, k_slabs)
    op = flash_prompt(q_slabs, k_slabs, v_slabs, tq=ATTN_TQ, cb=ATTN_CB, kc=ATTN_KC, hpi=ATTN_HPI)
    hp = mix_ffn(hp, op.reshape(bsz * seq, d), wo1, None, row(norm_ffn_g[1]), wgu, wd, row(norm_final_g),
                 layer=1, tm=PROMPT_TM, fc=FFN_CHUNK)
    hs = mix_ffn(hs, to_time_major(os_), wo1, None, row(norm_ffn_g[1]), wgu, wd, row(norm_final_g),
                 layer=1, tm=n_sample, fc=FFN_CHUNK)

    heads_t = lambda a: jnp.transpose(a.reshape(bsz, N_HEADS, HEAD_DIM, seq), (0, 3, 1, 2))[None]
    heads = lambda a: a.reshape(1, dbsz, dseq, N_HEADS, HEAD_DIM)
    return (hp.reshape(bsz, seq, d), to_batch_major(hs),
            conv_state_p[None], jnp.transpose(conv_state_s, (1, 0, 2))[None],
            heads_t(ktp), heads_t(vtp), jnp.transpose(lftp, (0, 2, 1))[None],
            heads(ks_b), heads(vs_b), jnp.transpose(lfts_b, (0, 2, 1))[None])
```

```python
import functools

import jax
import jax.numpy as jnp
from jax import lax
from jax.experimental import pallas as pl
from jax.experimental.pallas import tpu as pltpu

D_MODEL = 1024
N_HEADS = 16
HEAD_DIM = 64
N_PAIRS = N_HEADS // 2
LANES = 128
D_FF = 2816
CONV_WIDTH = 31
CONV_HIST = CONV_WIDTH - 1
HIST_PAD = 32
EPS = 1e-6
PAGE = 128
NEG = -1e30
VMEM_LIMIT_BYTES = 56 * 1024 * 1024

F32 = jnp.float32
BF16 = jnp.bfloat16
NT_DIMS = (((1,), (1,)), ((), ()))
TN_DIMS = (((0,), (0,)), ((), ()))


def _params(*sem):
    return pltpu.CompilerParams(dimension_semantics=sem, vmem_limit_bytes=VMEM_LIMIT_BYTES)


def _resident(shape, layer=None):
    nd = len(shape)
    if layer is None:
        return pl.BlockSpec(shape, lambda *_: (0,) * nd, pipeline_mode=pl.Buffered(1))
    return pl.BlockSpec((None,) + shape, lambda *_: (layer,) + (0,) * nd, pipeline_mode=pl.Buffered(1))


def _rmsnorm(x, g):
    return x * lax.rsqrt(jnp.mean(x * x, axis=-1, keepdims=True) + EPS) * g


def _sigmoid(x):
    return 1.0 / (1.0 + jnp.exp(-x))


def _log_sigmoid(x):
    return jnp.minimum(x, 0.0) - jnp.log1p(jnp.exp(-jnp.abs(x)))


def _layernorm_silu(y, g, b):
    mu = jnp.mean(y, axis=-1, keepdims=True)
    yc = y - mu
    var = jnp.mean(yc * yc, axis=-1, keepdims=True)
    yn = yc * lax.rsqrt(var + EPS) * g + b
    return yn * _sigmoid(yn)


def _dot(a, b):
    return jnp.dot(a, b, preferred_element_type=F32)


def _dot_nt(a, b):
    return lax.dot_general(a, b, NT_DIMS, preferred_element_type=F32)


def _cumsum_lanes(x):
    n = x.shape[1]
    tri = (lax.broadcasted_iota(jnp.int32, (n, n), 0) <= lax.broadcasted_iota(jnp.int32, (n, n), 1)).astype(BF16)
    hi = x.astype(BF16)
    rest = x - hi.astype(F32)
    mid = rest.astype(BF16)
    lo = (rest - mid.astype(F32)).astype(BF16)
    return _dot(hi, tri) + _dot(mid, tri) + _dot(lo, tri)


def _conv_in_kernel(h_ref, g_ref, w_ref, b_ref, u_ref):
    xn = _rmsnorm(h_ref[...], g_ref[...]).astype(BF16)
    hh = _dot(xn, w_ref[...]) + b_ref[...]
    u_ref[...] = hh[:, :D_MODEL] * _sigmoid(hh[:, D_MODEL:])


def conv_in(h, g, w_bf, b, *, tm):
    m = h.shape[0]
    return pl.pallas_call(
        _conv_in_kernel,
        out_shape=jax.ShapeDtypeStruct((m, D_MODEL), F32),
        grid=(m // tm,),
        in_specs=[pl.BlockSpec((tm, D_MODEL), lambda i: (i, 0)),
                  _resident((1, D_MODEL)),
                  _resident((D_MODEL, 2 * D_MODEL)),
                  _resident((1, 2 * D_MODEL))],
        out_specs=pl.BlockSpec((tm, D_MODEL), lambda i: (i, 0)),
        compiler_params=_params("parallel"),
        name="conv_in",
    )(h, g, w_bf, b)


SUBLANES = 8
CONV_RC = 16


def _conv_mid_step(t, u_ref, wdw_ref, bdw_ref, y_ref, st_ref, ubuf):
    tt = u_ref.shape[1]
    n_shift = ubuf.shape[1]

    @pl.when(t == 0)
    def _():
        ubuf[0, 0:HIST_PAD, :] = jnp.zeros((HIST_PAD, D_MODEL), F32)

    @pl.when(t > 0)
    def _():
        ubuf[0, 0:HIST_PAD, :] = ubuf[0, tt:tt + HIST_PAD, :]

    ubuf[0, HIST_PAD:HIST_PAD + tt, :] = u_ref[0]
    st_ref[0] = ubuf[0, tt + HIST_PAD - CONV_HIST:tt + HIST_PAD, :]
    for s in range(1, SUBLANES):
        ubuf[s, 0:n_shift - SUBLANES, :] = ubuf[0, s:s + n_shift - SUBLANES, :]

    off = HIST_PAD - CONV_HIST
    rc = min(tt, CONV_RC)

    def row_chunk(i, carry):
        r0 = pl.multiple_of(i * rc, rc)
        for l0 in range(0, D_MODEL, LANES):
            acc = jnp.broadcast_to(bdw_ref[:, l0:l0 + LANES], (rc, LANES))
            for j in range(CONV_WIDTH):
                shift, base = (j + off) % SUBLANES, (j + off) // SUBLANES * SUBLANES
                acc = acc + wdw_ref[j:j + 1, l0:l0 + LANES] * ubuf[shift, pl.ds(r0 + base, rc), l0:l0 + LANES]
            y_ref[0, pl.ds(r0, rc), l0:l0 + LANES] = acc
        return carry

    lax.fori_loop(0, tt // rc, row_chunk, 0)


def _conv_mid_sample_kernel(u_ref, hist_ref, wdw_ref, bdw_ref, lng_ref, lnb_ref, z_ref, st_ref, upad):
    dseq = u_ref.shape[0]
    upad[0:CONV_HIST] = hist_ref[...]
    upad[CONV_HIST:CONV_HIST + dseq] = u_ref[...]
    st_ref[...] = upad[dseq:dseq + CONV_HIST]
    for t in range(dseq):
        acc = jnp.broadcast_to(bdw_ref[...], upad.shape[1:])
        for j in range(CONV_WIDTH):
            acc = acc + wdw_ref[j:j + 1, :] * upad[t + j]
        z_ref[t] = _layernorm_silu(acc, lng_ref[...], lnb_ref[...])


def conv_mid_sample(u, hist, w_dw, b_dw, ln_g, ln_b):
    dseq, db, _ = u.shape
    whole = lambda shape: pl.BlockSpec(shape, lambda i: (0,) * len(shape))
    return pl.pallas_call(
        _conv_mid_sample_kernel,
        out_shape=(jax.ShapeDtypeStruct((dseq, db, D_MODEL), F32),
                   jax.ShapeDtypeStruct((CONV_HIST, db, D_MODEL), F32)),
        grid=(1,),
        in_specs=[whole(u.shape), whole(hist.shape), whole(w_dw.shape), whole(b_dw.shape), whole(ln_g.shape),
                  whole(ln_b.shape)],
        out_specs=(whole((dseq, db, D_MODEL)), whole((CONV_HIST, db, D_MODEL))),
        scratch_shapes=[pltpu.VMEM((CONV_HIST + dseq, db, D_MODEL), F32)],
        compiler_params=_params("arbitrary"),
        name="conv_mid_sample",
    )(u, hist, w_dw, b_dw, ln_g, ln_b)


def _mix_ffn_kernel(*refs, fc, has_ln, has_bias, final):
    h_ref, z_ref, wo_ref = refs[:3]
    refs = refs[3:]
    if has_ln:
        lng_ref, lnb_ref, refs = refs[0], refs[1], refs[2:]
    if has_bias:
        bo_ref, refs = refs[0], refs[1:]
    g_ref, wgu_ref, wd_ref = refs[:3]
    refs = refs[3:]
    if final:
        gf_ref, refs = refs[0], refs[1:]
    out_ref = refs[0]

    z = z_ref[...]
    if has_ln:
        z = _layernorm_silu(z, lng_ref[...], lnb_ref[...])
    h1 = h_ref[...] + _dot(z.astype(BF16), wo_ref[...])
    if has_bias:
        h1 = h1 + bo_ref[...]
    xn = _rmsnorm(h1, g_ref[...]).astype(BF16)
    acc = h1
    for c in range(D_FF // fc):
        gate = _dot(xn, wgu_ref[:, c * fc:(c + 1) * fc])
        up = _dot(xn, wgu_ref[:, D_FF + c * fc:D_FF + (c + 1) * fc])
        a = (gate * _sigmoid(gate) * up).astype(BF16)
        acc = acc + _dot(a, wd_ref[c * fc:(c + 1) * fc, :])
    out_ref[...] = _rmsnorm(acc, gf_ref[...]) if final else acc


def mix_ffn(h, z, wo_bf, bo, g_ffn, wgu_bf, wd_bf, g_final, *, layer, tm, fc, ln=None):
    m = h.shape[0]
    has_bias = bo is not None
    final = g_final is not None
    row = pl.BlockSpec((tm, D_MODEL), lambda i: (i, 0))
    args = [h, z, wo_bf]
    specs = [row, row, _resident((D_MODEL, D_MODEL))]
    if ln is not None:
        args += list(ln)
        specs += [_resident((1, D_MODEL)), _resident((1, D_MODEL))]
    if has_bias:
        args.append(bo)
        specs.append(_resident((1, D_MODEL)))
    args += [g_ffn, wgu_bf, wd_bf]
    specs += [_resident((1, D_MODEL)), _resident((D_MODEL, 2 * D_FF), layer), _resident((D_FF, D_MODEL), layer)]
    if final:
        args.append(g_final)
        specs.append(_resident((1, D_MODEL)))
    return pl.pallas_call(
        functools.partial(_mix_ffn_kernel, fc=fc, has_ln=ln is not None, has_bias=has_bias, final=final),
        out_shape=jax.ShapeDtypeStruct((m, D_MODEL), F32),
        grid=(m // tm,),
        in_specs=specs,
        out_specs=row,
        compiler_params=_params("parallel"),
        name="mix_ffn",
    )(*args)


C_ROW = HEAD_DIM
C_TERMS = 3
C_ROWS = 16
L_ROW = HEAD_DIM
V_ROWS = HEAD_DIM + 16
LOG2E = 1.4426950408889634


def _attn_proj_prompt_kernel(h_ref, g_ref, wt_ref, wft_ref, bf_ref, k_ref, v_ref, lf_ref, qs_ref, ks_ref, vs_ref):
    tm = h_ref.shape[1]
    xn = _rmsnorm(h_ref[0], g_ref[...]).astype(BF16)
    qkvt = _dot_nt(wt_ref[...], xn)
    k_ref[0] = qkvt[D_MODEL:2 * D_MODEL]
    v_ref[0] = qkvt[2 * D_MODEL:]
    lf_ref[0] = _log_sigmoid(_dot_nt(wft_ref[...], xn) + bf_ref[...])

    row = lax.broadcasted_iota(jnp.int32, (HEAD_DIM, 1), 0)
    q_tail = jnp.broadcast_to(jnp.where(row < C_TERMS, -1.0, 0.0), (HEAD_DIM, tm)).astype(BF16)
    k_tail = jnp.zeros((HEAD_DIM, tm), BF16)
    v_tail = jnp.broadcast_to(jnp.where(row[:V_ROWS - HEAD_DIM] == 0, 1.0, 0.0), (V_ROWS - HEAD_DIM, tm)).astype(BF16)
    for h in range(N_HEADS):
        head = slice(h * HEAD_DIM, (h + 1) * HEAD_DIM)
        qs_ref[0, h, :HEAD_DIM] = (qkvt[:D_MODEL][head] * (HEAD_DIM ** -0.5 * LOG2E)).astype(BF16)
        qs_ref[0, h, HEAD_DIM:] = q_tail
        ks_ref[0, h, :HEAD_DIM] = qkvt[D_MODEL:2 * D_MODEL][head].astype(BF16)
        ks_ref[0, h, HEAD_DIM:] = k_tail
        vs_ref[0, h, :HEAD_DIM] = qkvt[2 * D_MODEL:][head].astype(BF16)
        vs_ref[0, h, HEAD_DIM:] = v_tail


def attn_proj_prompt(h, g, wt_bf, wft_bf, bf_col, *, tm):
    b, s, _ = h.shape
    col = pl.BlockSpec((1, D_MODEL, tm), lambda i, j: (i, 0, j))
    slab = pl.BlockSpec((1, N_HEADS, LANES, tm), lambda i, j: (i, 0, 0, j))
    return pl.pallas_call(
        _attn_proj_prompt_kernel,
        out_shape=(jax.ShapeDtypeStruct((b, D_MODEL, s), F32), jax.ShapeDtypeStruct((b, D_MODEL, s), F32),
                   jax.ShapeDtypeStruct((b, N_HEADS, s), F32),
                   jax.ShapeDtypeStruct((b, N_HEADS, LANES, s), BF16),
                   jax.ShapeDtypeStruct((b, N_HEADS, LANES, s), BF16),
                   jax.ShapeDtypeStruct((b, N_HEADS, V_ROWS, s), BF16)),
        grid=(b, s // tm),
        in_specs=[pl.BlockSpec((1, tm, D_MODEL), lambda i, j: (i, j, 0)),
                  _resident((1, D_MODEL)),
                  _resident((3 * D_MODEL, D_MODEL)),
                  _resident((N_HEADS, D_MODEL)),
                  _resident((N_HEADS, 1))],
        out_specs=(col, col,
                   pl.BlockSpec((1, N_HEADS, tm), lambda i, j: (i, 0, j)),
                   slab, slab,
                   pl.BlockSpec((1, N_HEADS, V_ROWS, tm), lambda i, j: (i, 0, 0, j))),
        compiler_params=_params("parallel", "parallel"),
        name="attn_proj_prompt",
    )(h, g, wt_bf, wft_bf, bf_col)


def _attn_proj_kernel(h_ref, g_ref, wq_ref, wkvt_ref, wft_ref, bf_ref, q_ref, k_ref, v_ref, lf_ref):
    xn = _rmsnorm(h_ref[0], g_ref[...]).astype(BF16)
    q_ref[0] = (_dot(xn, wq_ref[...]) * (HEAD_DIM ** -0.5)).astype(BF16)
    kv = _dot_nt(wkvt_ref[...], xn).T
    k_ref[0] = kv[:, :D_MODEL]
    v_ref[0] = kv[:, D_MODEL:]
    lf_ref[0] = _log_sigmoid(_dot_nt(wft_ref[...], xn) + bf_ref[...])


def attn_proj(h, g, wq_bf, wkvt_bf, wft_bf, bf_col, *, tm):
    b, s, _ = h.shape
    row = pl.BlockSpec((1, tm, D_MODEL), lambda i, j: (i, j, 0))
    out_shape = [jax.ShapeDtypeStruct((b, s, D_MODEL), BF16),
                 jax.ShapeDtypeStruct((b, s, D_MODEL), F32), jax.ShapeDtypeStruct((b, s, D_MODEL), F32),
                 jax.ShapeDtypeStruct((b, N_HEADS, s), F32)]
    out_specs = [row, row, row, pl.BlockSpec((1, N_HEADS, tm), lambda i, j: (i, 0, j))]
    return pl.pallas_call(
        _attn_proj_kernel,
        out_shape=tuple(out_shape),
        grid=(b, s // tm),
        in_specs=[row,
                  _resident((1, D_MODEL)),
                  _resident((D_MODEL, D_MODEL)),
                  _resident((2 * D_MODEL, D_MODEL)),
                  _resident((N_HEADS, D_MODEL)),
                  _resident((N_HEADS, 1))],
        out_specs=tuple(out_specs),
        compiler_params=_params("parallel", "parallel"),
        name="attn_proj",
    )(h, g, wq_bf, wkvt_bf, wft_bf, bf_col)


CUMSUM_CHUNK = 256


def _cumsum_kernel(x_ref, slab_hbm_ref, c_ref):
    del slab_hbm_ref
    s = x_ref.shape[2]
    row = lax.broadcasted_iota(jnp.int32, (C_ROWS, CUMSUM_CHUNK), 0)
    carry = jnp.zeros((N_HEADS, 1), F32)
    for j in range(s // CUMSUM_CHUNK):
        sl = slice(j * CUMSUM_CHUNK, (j + 1) * CUMSUM_CHUNK)
        local = _cumsum_lanes(x_ref[0, :, sl])
        c = (local + carry) * LOG2E
        carry = carry + local[:, CUMSUM_CHUNK - 1:CUMSUM_CHUNK]
        hi = c.astype(BF16).astype(F32)
        mid = (c - hi).astype(BF16).astype(F32)
        lo = c - hi - mid
        for h in range(N_HEADS):
            rows = jnp.where(row == 0, hi[h:h + 1], jnp.where(row == 1, mid[h:h + 1],
                                                               jnp.where(row == 2, lo[h:h + 1], 0.0)))
            c_ref[0, h, :, sl] = rows.astype(BF16)


def cumsum_into_slabs(lft, k_slabs):
    b, h, s = lft.shape
    return pl.pallas_call(
        _cumsum_kernel,
        out_shape=jax.ShapeDtypeStruct(k_slabs.shape, k_slabs.dtype),
        grid=(b,),
        in_specs=[pl.BlockSpec((1, h, s), lambda i: (i, 0, 0)),
                  pl.BlockSpec(memory_space=pl.ANY)],
        out_specs=pl.BlockSpec((1, h, C_ROWS, s), lambda i: (i, 0, C_ROW // C_ROWS, 0)),
        input_output_aliases={1: 0},
        compiler_params=_params("parallel"),
        name="cumsum_into_slabs",
    )(lft, k_slabs)


def _flash_kernel(qi_ref, ki_ref, q_ref, k_ref, v_ref, o_ref, acc_ref, m_ref, *, tq, cb, kc, hpi):
    t = pl.program_id(1)
    qi = qi_ref[t]
    ki = ki_ref[t]

    @pl.when(ki == 0)
    def _():
        m_ref[...] = jnp.full(m_ref.shape, NEG, F32)
        acc_ref[...] = jnp.zeros(acc_ref.shape, F32)

    def sweep(diagonal):
        def heads(j, carry):
            chains = [(j * hpi + hh, c) for hh in range(hpi) for c in range(tq // cb)]
            cols = [slice(c * cb, (c + 1) * cb) for _, c in chains]
            nk = [(c + 1) * cb if diagonal else tq for _, c in chains]
            n = range(len(chains))
            for k0 in range(0, tq, kc):
                live = [i for i in n if k0 < nk[i]]
                s = {i: lax.dot_general(k_ref[0, chains[i][0], :, k0:k0 + kc], q_ref[0, chains[i][0], :, cols[i]],
                                        TN_DIMS, preferred_element_type=F32)
                     for i in live}
                for i in live:
                    si = s[i]
                    if diagonal and k0 + kc > chains[i][1] * cb:
                        kpos = lax.broadcasted_iota(jnp.int32, si.shape, 0) + k0
                        qpos = lax.broadcasted_iota(jnp.int32, si.shape, 1) + chains[i][1] * cb
                        si = jnp.where(kpos <= qpos, si, NEG)
                    h = chains[i][0]
                    m_prev = m_ref[h, :, cols[i]]
                    m_new = jnp.maximum(m_prev, jnp.max(si, axis=0, keepdims=True))
                    pr = jnp.exp2(si - m_new[:1])
                    acc_ref[h, :, cols[i]] = (jnp.exp2(m_prev - m_new)[:1] * acc_ref[h, :, cols[i]]
                                              + _dot(v_ref[0, h, :, k0:k0 + kc], pr.astype(BF16)))
                    m_ref[h, :, cols[i]] = m_new
            return carry

        lax.fori_loop(0, N_HEADS // hpi, heads, 0)

    @pl.when(ki < qi)
    def _():
        sweep(False)

    @pl.when(ki == qi)
    def _():
        sweep(True)
        for p in range(N_PAIRS):
            pair = [acc_ref[2 * p + hh] for hh in range(2)]
            pair = [a[:HEAD_DIM] / a[L_ROW:L_ROW + 1] for a in pair]
            o_ref[0, :, p * LANES:(p + 1) * LANES] = jnp.concatenate(pair, axis=0).T.astype(o_ref.dtype)


def flash_prompt(q_slabs, k_slabs, v_slabs, *, tq, cb, kc, hpi):
    b, _, _, s = q_slabs.shape
    nq = s // tq
    qi = jnp.asarray([i for i in range(nq) for _ in range(i + 1)], jnp.int32)
    ki = jnp.asarray([j for i in range(nq) for j in range(i + 1)], jnp.int32)
    grid_spec = pltpu.PrefetchScalarGridSpec(
        num_scalar_prefetch=2,
        grid=(b, int(qi.shape[0])),
        in_specs=[pl.BlockSpec((1, N_HEADS, LANES, tq), lambda i, t, qi, ki: (i, 0, 0, qi[t])),
                  pl.BlockSpec((1, N_HEADS, LANES, tq), lambda i, t, qi, ki: (i, 0, 0, ki[t])),
                  pl.BlockSpec((1, N_HEADS, V_ROWS, tq), lambda i, t, qi, ki: (i, 0, 0, ki[t]))],
        out_specs=pl.BlockSpec((1, tq, D_MODEL), lambda i, t, qi, ki: (i, qi[t], 0)),
        scratch_shapes=[pltpu.VMEM((N_HEADS, V_ROWS, tq), F32),
                        pltpu.VMEM((N_HEADS, SUBLANES, tq), F32)],
    )
    return pl.pallas_call(
        functools.partial(_flash_kernel, tq=tq, cb=cb, kc=kc, hpi=hpi),
        out_shape=jax.ShapeDtypeStruct((b, s, D_MODEL), BF16),
        grid_spec=grid_spec,
        compiler_params=_params("parallel", "arbitrary"),
        name="flash_prompt",
    )(qi, ki, q_slabs, k_slabs, v_slabs)


NEW_COLS = 16
PAGES_PER_DOT = 4


def _sample_attn_step(g, last, q_ref, kn_ref, vn_ref, lfn_ref, kt_refs, vt_refs, lp_refs,
                      o_ref, qbd_ref, acc_ref, m_ref, l_ref, carry_ref):
    dec_seq = q_ref.shape[1]
    n_rows = dec_seq * N_HEADS

    lane_d = lax.broadcasted_iota(jnp.int32, (N_HEADS, D_MODEL), 1)
    head_d = lax.broadcasted_iota(jnp.int32, (N_HEADS, D_MODEL), 0)
    head_mask = (lane_d // HEAD_DIM == head_d).astype(F32)

    @pl.when(g == 0)
    def _():
        qf = q_ref[0].astype(F32)
        for t in range(dec_seq):
            qbd_ref[t * N_HEADS:(t + 1) * N_HEADS, :] = (head_mask * qf[t:t + 1, :]).astype(BF16)
        m_ref[...] = jnp.full(m_ref.shape, NEG, F32)
        l_ref[...] = jnp.zeros(l_ref.shape, F32)
        acc_ref[...] = jnp.zeros(acc_ref.shape, F32)
        carry_ref[...] = jnp.zeros(carry_ref.shape, F32)

    def attend(scores, lf_rows, mask, pvs):
        n = lf_rows[0].shape[1]
        per = len(lf_rows) // len(scores)
        local_all = _cumsum_lanes(jnp.concatenate(lf_rows, axis=0))
        carry = carry_ref[...]
        m_prev = m_ref[...]
        m_new = m_prev
        s = []
        for i, sc in enumerate(scores):
            cs = []
            for r in range(i * per, (i + 1) * per):
                local = jnp.concatenate([local_all[r * N_HEADS:(r + 1) * N_HEADS]] * dec_seq, axis=0)
                cs.append(local + carry[:, :n])
                carry = carry + local[:, n - 1:n]
            si = sc - jnp.concatenate(cs, axis=1)
            if mask is not None:
                si = jnp.where(mask, si, NEG)
            m_new = jnp.maximum(m_new, jnp.max(si, axis=1, keepdims=True))
            s.append(si)
        alpha = jnp.exp(m_prev - m_new)
        l_new = alpha * l_ref[...]
        acc = alpha[:, :1] * acc_ref[...]
        for si, pv in zip(s, pvs):
            pr = jnp.exp(si - jnp.concatenate([m_new[:, :n]] * per, axis=1))
            l_new = l_new + jnp.sum(pr, axis=1, keepdims=True)
            acc = acc + pv(pr.astype(BF16))
        carry_ref[...] = carry
        m_ref[...] = m_new
        l_ref[...] = l_new
        acc_ref[...] = acc

    def pages(refs, i):
        return jnp.concatenate([r[...].astype(BF16) for r in refs[i:i + PAGES_PER_DOT]], axis=1)

    qbd = qbd_ref[...]
    firsts = range(0, len(kt_refs), PAGES_PER_DOT)
    attend([_dot(qbd, pages(kt_refs, i)) for i in firsts], [lp[...] for lp in lp_refs], None,
           [lambda pr, i=i: _dot_nt(pr, pages(vt_refs, i)) for i in firsts])

    @pl.when(last)
    def _():
        r = lax.broadcasted_iota(jnp.int32, (n_rows, NEW_COLS), 0)
        col = lax.broadcasted_iota(jnp.int32, (n_rows, NEW_COLS), 1)
        attend([_dot_nt(qbd_ref[...], kn_ref[0].astype(BF16))], [lfn_ref[0]], col <= r // N_HEADS,
               [lambda pr: _dot(pr, vn_ref[0].astype(BF16))])
        o = acc_ref[...] / l_ref[:, :1]
        rows = [jnp.sum(o[t * N_HEADS:(t + 1) * N_HEADS, :] * head_mask, axis=0, keepdims=True)
                for t in range(dec_seq)]
        o_ref[0] = jnp.concatenate(rows, axis=0)


def _sample_attn_conv_kernel(pt_ref, q_ref, kn_ref, vn_ref, lfn_ref, *refs, group, conv_tiles):
    kt_refs, vt_refs, lp_refs = refs[:group], refs[group:2 * group], refs[2 * group:3 * group]
    conv_in_refs = refs[3 * group:3 * group + 3]
    o_ref, z_ref, st_ref = refs[3 * group + 3:3 * group + 6]
    attn_scratch = refs[3 * group + 6:3 * group + 11]
    conv_scratch = refs[3 * group + 11:]
    g = pl.program_id(1)
    step = pl.program_id(0) * pl.num_programs(1) + g
    _sample_attn_step(g, g == pl.num_programs(1) - 1, q_ref, kn_ref, vn_ref, lfn_ref, kt_refs, vt_refs, lp_refs,
                      o_ref, *attn_scratch)
    _conv_mid_step(step % conv_tiles, *conv_in_refs, z_ref, st_ref, *conv_scratch)


def sample_attention_and_conv_mid(page_ids, q, k_new, v_new, lft_new, cache_kt, cache_vt, cache_lft,
                                  u, w_dw, b_dw, *, group):
    db, n_pages = page_ids.shape
    dec_seq = q.shape[1]
    n_rows = dec_seq * N_HEADS
    n_groups = n_pages // group
    bsz, seq, _ = u.shape
    tt = bsz * seq // (db * n_groups)
    conv_tiles = seq // tt
    assert tt * db * n_groups == bsz * seq and conv_tiles * tt == seq and tt % SUBLANES == 0 and tt >= HIST_PAD

    def page_spec(rows, i):
        return pl.BlockSpec((None, rows, PAGE), lambda b, g, pt: (pt[b, g * group + i], 0, 0))

    def per_batch(rows, width):
        return pl.BlockSpec((1, rows, width), lambda b, g, pt: (b, 0, 0))

    conv_tile = pl.BlockSpec((1, tt, D_MODEL), lambda b, g, pt: ((b * n_groups + g) // conv_tiles,
                                                                  (b * n_groups + g) % conv_tiles, 0))
    grid_spec = pltpu.PrefetchScalarGridSpec(
        num_scalar_prefetch=1,
        grid=(db, n_groups),
        in_specs=([per_batch(dec_seq, D_MODEL), per_batch(NEW_COLS, D_MODEL), per_batch(NEW_COLS, D_MODEL),
                   per_batch(N_HEADS, NEW_COLS)]
                  + [page_spec(D_MODEL, i) for i in range(group)]
                  + [page_spec(D_MODEL, i) for i in range(group)]
                  + [page_spec(N_HEADS, i) for i in range(group)]
                  + [conv_tile, _resident((CONV_WIDTH, D_MODEL)), _resident((1, D_MODEL))]),
        out_specs=(per_batch(dec_seq, D_MODEL), conv_tile,
                   pl.BlockSpec((1, CONV_HIST, D_MODEL), lambda b, g, pt: ((b * n_groups + g) // conv_tiles, 0, 0))),
        scratch_shapes=[pltpu.VMEM((n_rows, D_MODEL), BF16),
                        pltpu.VMEM((n_rows, D_MODEL), F32),
                        pltpu.VMEM((n_rows, LANES), F32),
                        pltpu.VMEM((n_rows, LANES), F32),
                        pltpu.VMEM((n_rows, LANES), F32),
                        pltpu.VMEM((SUBLANES, HIST_PAD + tt, D_MODEL), F32)],
    )
    return pl.pallas_call(
        functools.partial(_sample_attn_conv_kernel, group=group, conv_tiles=conv_tiles),
        out_shape=(jax.ShapeDtypeStruct((db, dec_seq, D_MODEL), F32),
                   jax.ShapeDtypeStruct((bsz, seq, D_MODEL), F32),
                   jax.ShapeDtypeStruct((bsz, CONV_HIST, D_MODEL), F32)),
        grid_spec=grid_spec,
        compiler_params=_params("arbitrary", "arbitrary"),
        name="sample_attention_and_conv_mid",
    )(page_ids, q, k_new, v_new, lft_new, *([cache_kt] * group), *([cache_vt] * group), *([cache_lft] * group),
      u, w_dw, b_dw)


PROMPT_TM = 512
FFN_CHUNK = 2816
ATTN_TQ = 512
ATTN_CB = 256
ATTN_KC = 256
ATTN_HPI = 16
PAGE_GROUP = 16


def kernel(x_prompt, x_sample, state_conv, cache_k, cache_v, cache_logf, page_table, norm_mix_g, norm_ffn_g,
           norm_final_g, conv_w_in, conv_b_in, conv_w_dw, conv_b_dw, conv_ln_g, conv_ln_b, conv_w_out, conv_b_out,
           attn_w_in, attn_b_f, attn_w_out, ffn_w_gu, ffn_w_down):
    bsz, seq, d = x_prompt.shape
    dbsz, dseq, _ = x_sample.shape
    n_sample = dbsz * dseq
    row = lambda a: a.reshape(1, -1)
    to_time_major = lambda a: jnp.transpose(a.reshape(dbsz, dseq, -1), (1, 0, 2)).reshape(n_sample, -1)
    to_batch_major = lambda a: jnp.transpose(a.reshape(dseq, dbsz, -1), (1, 0, 2))

    w_in = conv_w_in[0].astype(BF16)
    w_out = conv_w_out[0].astype(BF16)
    wgu = ffn_w_gu.astype(BF16)
    wd = ffn_w_down.astype(BF16)
    conv_args = (conv_w_dw[0], row(conv_b_dw[0]), row(conv_ln_g[0]), row(conv_ln_b[0]))
    w_in_t = jnp.transpose(attn_w_in[0]).astype(BF16)
    wq = attn_w_in[0, :, :d].astype(BF16)
    wft = w_in_t[3 * d:]
    bf_col = attn_b_f[0].reshape(N_HEADS, 1)
    wo1 = attn_w_out[0].astype(BF16)

    hp = x_prompt.reshape(bsz * seq, d)
    hs = to_time_major(x_sample)

    us = conv_in(hs, row(norm_mix_g[0]), w_in, row(conv_b_in[0]), tm=n_sample)
    zs, conv_state_s = conv_mid_sample(us.reshape(dseq, dbsz, d), jnp.transpose(state_conv[0], (1, 0, 2)),
                                       *conv_args)
    hs = mix_ffn(hs, zs.reshape(n_sample, d), w_out, row(conv_b_out[0]), row(norm_ffn_g[0]), wgu, wd, None,
                 layer=0, tm=n_sample, fc=FFN_CHUNK)
    qs, ks, vs, lfts = attn_proj(hs.reshape(1, n_sample, d), row(norm_mix_g[1]), wq, w_in_t[d:3 * d], wft, bf_col,
                                 tm=n_sample)
    ks_b = to_batch_major(ks[0])
    vs_b = to_batch_major(vs[0])
    lfts_b = jnp.transpose(lfts.reshape(N_HEADS, dseq, dbsz), (2, 0, 1))
    pad_rows = lambda a: jnp.pad(a, ((0, 0), (0, NEW_COLS - dseq), (0, 0)))
    cache_kt = jnp.transpose(cache_k, (0, 1, 3, 4, 2)).reshape(-1, d, PAGE)
    cache_vt = jnp.transpose(cache_v, (0, 1, 3, 4, 2)).reshape(-1, d, PAGE)
    cache_lft = jnp.transpose(cache_logf, (0, 1, 3, 2)).reshape(-1, N_HEADS, PAGE)

    up = conv_in(hp, row(norm_mix_g[0]), w_in, row(conv_b_in[0]), tm=PROMPT_TM)
    os_, zp, conv_state_p = sample_attention_and_conv_mid(
        page_table, to_batch_major(qs[0]), pad_rows(ks_b), pad_rows(vs_b),
        jnp.pad(lfts_b, ((0, 0), (0, 0), (0, NEW_COLS - dseq))), cache_kt, cache_vt, cache_lft,
        up.reshape(bsz, seq, d), *conv_args[:2], group=PAGE_GROUP)
    hp = mix_ffn(hp, zp.reshape(bsz * seq, d), w_out, row(conv_b_out[0]), row(norm_ffn_g[0]), wgu, wd, None,
                 layer=0, tm=PROMPT_TM, fc=FFN_CHUNK, ln=conv_args[2:])

    ktp, vtp, lftp, q_slabs, k_slabs, v_slabs = attn_proj_prompt(
        hp.reshape(bsz, seq, d), row(norm_mix_g[1]), w_in_t[:3 * d], wft, bf_col, tm=PROMPT_TM)
    k_slabs = cumsum_into_slabs(lftp, k_slabs)
    op = flash_prompt(q_slabs, k_slabs, v_slabs, tq=ATTN_TQ, cb=ATTN_CB, kc=ATTN_KC, hpi=ATTN_HPI)
    hp = mix_ffn(hp, op.reshape(bsz * seq, d), wo1, None, row(norm_ffn_g[1]), wgu, wd, row(norm_final_g),
                 layer=1, tm=PROMPT_TM, fc=FFN_CHUNK)
    hs = mix_ffn(hs, to_time_major(os_), wo1, None, row(norm_ffn_g[1]), wgu, wd, row(norm_final_g),
                 layer=1, tm=n_sample, fc=FFN_CHUNK)

    heads_t = lambda a: jnp.transpose(a.reshape(bsz, N_HEADS, HEAD_DIM, seq), (0, 3, 1, 2))[None]
    heads = lambda a: a.reshape(1, dbsz, dseq, N_HEADS, HEAD_DIM)
    return (hp.reshape(bsz, seq, d), to_batch_major(hs),
            conv_state_p[None], jnp.transpose(conv_state_s, (1, 0, 2))[None],
            heads_t(ktp), heads_t(vtp), jnp.transpose(lftp, (0, 2, 1))[None],
            heads(ks_b), heads(vs_b), jnp.transpose(lfts_b, (0, 2, 1))[None])
```
